```python
import math
import jax
import jax.numpy as jnp
from jax import lax
import numpy as np

D_MODEL = 1024
BATCH = 8
SEQ = 2048
DEPTH = 1

GRID_W = 64
CTX_LEN = 256
D_MIX = D_MODEL
HY_WIDTH = D_MIX // 2
RET_WIDTH = D_MIX - HY_WIDTH
RET_HEADS = 4
RET_HEAD_DIM = RET_WIDTH // RET_HEADS
RET_CHUNK = 128
RET_DECAY_OFFSET_F = 5.0
RET_DECAY_OFFSET_B = 5.5
ROPE_AXIS_DIM = RET_HEAD_DIM // 4
ROPE_BASE = 10000.0
HY_EMB_DIM = 33
HY_FILTER_HIDDEN = 64
HY_DECAY_TARGET = 1e-2
HY_FAST_PCT = 0.3
HY_SLOW_PCT = 1.5
HY_DECAY_SHIFT = 0.05
PROJ_COLS = 3 * HY_WIDTH + 4 * RET_WIDTH
N_GROUPS = 4
EXPERTS_PER_GROUP = 4
TOP_K_WITHIN = 2
EXPERT_HIDDEN = D_MODEL // 2
N_MOD = 6
EPS = 1e-6
F32 = jnp.float32

kernel_name = 'hybrid_hyena_retention_hmoe_dit'


def rmsnorm(x, w):
    xf = x.astype(F32)
    y = xf * lax.rsqrt(jnp.mean(xf * xf, axis=-1, keepdims=True) + EPS)
    return (y * w.astype(F32)).astype(x.dtype)


def modulate(h, shift, scale):
    return h * (1.0 + scale) + shift


def short_conv3(u, w, b):
    up = jnp.pad(u, ((0, 0), (1, 1), (0, 0)))
    return up[:, :-2] * w[0] + up[:, 1:-1] * w[1] + up[:, 2:] * w[2] + b


def hyena_filters(n_tok, lp):
    t = jnp.linspace(0.0, 1.0, n_tok, dtype=F32)[:, None]
    n_bands = (HY_EMB_DIM - 1) // 2
    bands = jnp.linspace(1e-4, n_bands - 1, n_bands, dtype=F32)
    ang = (2.0 * math.pi / n_tok) * jnp.arange(n_tok, dtype=F32)[:, None] * bands[None, :]
    z = jnp.concatenate([t, jnp.cos(ang), -jnp.sin(ang)], axis=-1)
    freq = lp['hy_f_freq'].astype(F32)
    h = jnp.sin(freq * (z @ lp['hy_f_w1'].astype(F32) + lp['hy_f_b1'].astype(F32)))
    h = jnp.sin(freq * (h @ lp['hy_f_w2'].astype(F32) + lp['hy_f_b2'].astype(F32)))
    h = jnp.sin(freq * (h @ lp['hy_f_w3'].astype(F32) + lp['hy_f_b3'].astype(F32)))
    h = h @ lp['hy_f_wout'].astype(F32)
    max_decay = math.log(HY_DECAY_TARGET) / HY_FAST_PCT
    min_decay = math.log(HY_DECAY_TARGET) / HY_SLOW_PCT
    deltas = jnp.abs(jnp.linspace(min_decay, max_decay, HY_WIDTH, dtype=F32))
    window = jnp.exp(-t * deltas[None, :]) + HY_DECAY_SHIFT
    h = h.reshape(n_tok, 2, HY_WIDTH) * window[:, None, :]
    return h[:, 0], h[:, 1]


def bidir_long_conv(u, h_fwd, h_bwd, skip):
    n_tok = u.shape[1]
    n_fft = 2 * n_tok
    filt = jnp.concatenate([h_fwd, jnp.zeros_like(h_fwd[:1]), h_bwd[:0:-1]], axis=0)
    uf = u.astype(F32)
    y = jnp.fft.irfft(jnp.fft.rfft(uf, n=n_fft, axis=1) * jnp.fft.rfft(filt, n=n_fft, axis=0)[None],
                      n=n_fft, axis=1)[:, :n_tok]
    return (y + uf * skip.astype(F32)).astype(u.dtype)


def hyena_mixer(p_hy, lp):
    u = short_conv3(p_hy, lp['hy_conv_w'], lp['hy_conv_b'])
    x0, x1, v = jnp.split(u, 3, axis=-1)
    h_fwd, h_bwd = hyena_filters(u.shape[1], lp)
    y = x0 * bidir_long_conv(v * x1, h_fwd, h_bwd, lp['hy_skip'])
    return rmsnorm(y, lp['hy_out_norm'])


def split_heads(t):
    b, n_tok, _ = t.shape
    return t.reshape(b, n_tok, RET_HEADS, RET_HEAD_DIM).transpose(0, 2, 1, 3)


def grid_positions(n_tok):
    ROWS = n_tok // GRID_W
    rows = jnp.repeat(jnp.arange(ROWS, dtype=F32), GRID_W)
    cols = jnp.tile(jnp.arange(GRID_W, dtype=F32), ROWS)
    return rows, cols


def axial_rope(t, rows, cols):
    n_freq = ROPE_AXIS_DIM // 2
    freqs = ROPE_BASE ** (-jnp.arange(n_freq, dtype=F32) / n_freq)

    def rot(seg, pos):
        ang = pos[:, None] * freqs[None, :]
        cs, sn = jnp.cos(ang), jnp.sin(ang)
        a, b = seg[..., :n_freq], seg[..., n_freq:]
        return jnp.concatenate([a * cs - b * sn, a * sn + b * cs], axis=-1)

    return jnp.concatenate([rot(t[..., :ROPE_AXIS_DIM], rows),
                            rot(t[..., ROPE_AXIS_DIM:2 * ROPE_AXIS_DIM], cols),
                            t[..., 2 * ROPE_AXIS_DIM:]], axis=-1)


def log_gammas(offset):
    return jnp.log1p(-jnp.exp2(-(offset + jnp.arange(RET_HEADS, dtype=F32))))


def decay_tables(log_gamma):
    i = jnp.arange(RET_CHUNK, dtype=F32)
    diff = i[:, None] - i[None, :]
    d_intra = jnp.where(diff >= 0, jnp.exp(jnp.maximum(diff, 0.0)[None] * log_gamma[:, None, None]), 0.0)
    xi = jnp.exp((i + 1.0)[None, :] * log_gamma[:, None])
    zeta = jnp.exp((RET_CHUNK - 1.0 - i)[None, :] * log_gamma[:, None])
    g_chunk = jnp.exp(RET_CHUNK * log_gamma)
    return d_intra, xi, zeta, g_chunk


def retention_scan(q, k, v, log_gamma, s0):
    b, nh, n_tok, dh = q.shape
    n_chunks = n_tok // RET_CHUNK
    d_intra, xi, zeta, g_chunk = decay_tables(log_gamma)

    def chunks(t):
        return t.reshape(b, nh, n_chunks, RET_CHUNK, dh).transpose(2, 0, 1, 3, 4)

    def step(s, qkv):
        qc, kc, vc = qkv
        scores = jnp.einsum('bhid,bhjd->bhij', qc, kc) * d_intra[None]
        inner = jnp.einsum('bhij,bhjd->bhid', scores, vc)
        cross = jnp.einsum('bhid,bhde->bhie', qc, s) * xi[None, :, :, None]
        s_new = s * g_chunk[None, :, None, None] + jnp.einsum('bhjd,bhje->bhde', kc * zeta[None, :, :, None], vc)
        return s_new, inner + cross

    s_fin, o = lax.scan(step, s0, (chunks(q), chunks(k), chunks(v)))
    return o.transpose(1, 2, 0, 3, 4).reshape(b, nh, n_tok, dh), s_fin


def decayed_state(k, v, log_gamma):
    n_tok = k.shape[2]
    w = jnp.exp((n_tok - 1.0 - jnp.arange(n_tok, dtype=F32))[None, :] * log_gamma[:, None])
    return jnp.einsum('hl,bhld,bhle->bhde', w, k, v)


def head_group_norm(o, w):
    mu = jnp.mean(o, axis=-1, keepdims=True)
    var = jnp.mean(jnp.square(o - mu), axis=-1, keepdims=True)
    on = (o - mu) * lax.rsqrt(var + EPS)
    b, nh, n_tok, dh = o.shape
    return on.transpose(0, 2, 1, 3).reshape(b, n_tok, nh * dh) * w.astype(F32)


def token_mixer(h, lp, s0_f, s0_b, on_grid):
    proj = h @ lp['w_in']
    y_hy = hyena_mixer(proj[..., :3 * HY_WIDTH], lp)
    q, k, v, g = jnp.split(proj[..., 3 * HY_WIDTH:], 4, axis=-1)
    q, k, v = (split_heads(t).astype(F32) for t in (q, k, v))
    k = k * RET_HEAD_DIM ** -0.5
    if on_grid:
        rows, cols = grid_positions(h.shape[1])
        q = axial_rope(q, rows, cols)
        k = axial_rope(k, rows, cols)
    o_f, s_f = retention_scan(q, k, v, log_gammas(RET_DECAY_OFFSET_F), s0_f)
    o_b, s_b = retention_scan(q[:, :, ::-1], k[:, :, ::-1], v[:, :, ::-1], log_gammas(RET_DECAY_OFFSET_B), s0_b)
    o = o_f + o_b[:, :, ::-1]
    y_ret = (jax.nn.silu(g.astype(F32)) * head_group_norm(o, lp['ret_gn_w'])).astype(h.dtype)
    y = jnp.concatenate([y_hy, y_ret], axis=-1) @ lp['w_out']
    return y, s_f, s_b


def context_states(hc, lp):
    w_in = lp['w_in']
    k0 = 3 * HY_WIDTH + RET_WIDTH
    k = split_heads(hc @ w_in[:, k0:k0 + RET_WIDTH]).astype(F32) * RET_HEAD_DIM ** -0.5
    v = split_heads(hc @ w_in[:, k0 + RET_WIDTH:k0 + 2 * RET_WIDTH]).astype(F32)
    s_f = decayed_state(k, v, log_gammas(RET_DECAY_OFFSET_F))
    s_b = decayed_state(k[:, :, ::-1], v[:, :, ::-1], log_gammas(RET_DECAY_OFFSET_B))
    return s_f, s_b


def hier_moe(h, lp):
    b, n_tok, d = h.shape
    t = h.reshape(-1, d)
    tf = t.astype(F32)
    lg = tf @ lp['router_g_w'].astype(F32) + lp['router_g_b'].astype(F32)
    pg = jax.nn.softmax(lg, axis=-1)
    onehot_g = jax.nn.one_hot(jnp.argmax(lg, axis=-1), N_GROUPS, dtype=F32)
    p_sel = jnp.sum(pg * onehot_g, axis=-1)
    le = (tf @ lp['router_e_w'].astype(F32) + lp['router_e_b'].astype(F32)).reshape(-1, N_GROUPS, EXPERTS_PER_GROUP)
    pe = jax.nn.softmax(jnp.einsum('tg,tge->te', onehot_g, le), axis=-1)
    top_v, top_i = lax.top_k(pe, TOP_K_WITHIN)
    top_v = top_v / jnp.sum(top_v, axis=-1, keepdims=True)
    w_within = jnp.einsum('tk,tke->te', top_v, jax.nn.one_hot(top_i, EXPERTS_PER_GROUP, dtype=F32))
    combine = (onehot_g[:, :, None] * (p_sel[:, None] * w_within)[:, None, :]).astype(t.dtype)
    y = jnp.zeros_like(t)
    for gi in range(N_GROUPS):
        a = jnp.einsum('td,edh->teh', t, lp['exp_w1'][gi])
        u = jnp.einsum('td,edh->teh', t, lp['exp_w3'][gi])
        hid = jax.nn.silu(a) * u * combine[:, gi, :, None]
        y = y + jnp.einsum('teh,ehd->td', hid, lp['exp_w2'][gi])
    return y.reshape(b, n_tok, d)


def setup_inputs(seed: int = 0) -> dict:
    key = jax.random.key(seed)
    ks = jax.random.split(key, 32)
    D = D_MODEL

    def nrm(k, shape, scale):
        return scale * jax.random.normal(k, shape, F32)

    return {
        'x': nrm(ks[0], (BATCH, SEQ, D), 1.0),
        'c': nrm(ks[1], (BATCH, D), 1.0),
        'ctx': nrm(ks[2], (BATCH, CTX_LEN, D), 1.0),
        'c_ctx': nrm(ks[3], (D,), 1.0),
        'ada_w': nrm(ks[4], (DEPTH, D, N_MOD * D), 0.5 * D ** -0.5),
        'ada_b': nrm(ks[5], (DEPTH, N_MOD * D), 0.02),
        'norm1_w': 1.0 + nrm(ks[6], (DEPTH, D), 0.05),
        'w_in': nrm(ks[7], (DEPTH, D, PROJ_COLS), D ** -0.5),
        'hy_conv_w': nrm(ks[8], (DEPTH, 3, 3 * HY_WIDTH), 3 ** -0.5),
        'hy_conv_b': nrm(ks[9], (DEPTH, 3 * HY_WIDTH), 0.02),
        'hy_f_w1': nrm(ks[10], (DEPTH, HY_EMB_DIM, HY_FILTER_HIDDEN), HY_EMB_DIM ** -0.5),
        'hy_f_b1': nrm(ks[11], (DEPTH, HY_FILTER_HIDDEN), 0.1),
        'hy_f_w2': nrm(ks[12], (DEPTH, HY_FILTER_HIDDEN, HY_FILTER_HIDDEN), HY_FILTER_HIDDEN ** -0.5),
        'hy_f_b2': nrm(ks[13], (DEPTH, HY_FILTER_HIDDEN), 0.1),
        'hy_f_w3': nrm(ks[14], (DEPTH, HY_FILTER_HIDDEN, HY_FILTER_HIDDEN), HY_FILTER_HIDDEN ** -0.5),
        'hy_f_b3': nrm(ks[15], (DEPTH, HY_FILTER_HIDDEN), 0.1),
        'hy_f_wout': nrm(ks[16], (DEPTH, HY_FILTER_HIDDEN, 2 * HY_WIDTH), HY_FILTER_HIDDEN ** -0.5),
        'hy_f_freq': 1.0 + nrm(ks[17], (DEPTH, HY_FILTER_HIDDEN), 0.1),
        'hy_skip': nrm(ks[18], (DEPTH, HY_WIDTH), 0.5),
        'hy_out_norm': 1.0 + nrm(ks[19], (DEPTH, HY_WIDTH), 0.05),
        'ret_gn_w': 1.0 + nrm(ks[20], (DEPTH, RET_WIDTH), 0.05),
        'w_out': nrm(ks[21], (DEPTH, D_MIX, D), D_MIX ** -0.5),
        'norm2_w': 1.0 + nrm(ks[22], (DEPTH, D), 0.05),
        'router_g_w': nrm(ks[23], (DEPTH, D, N_GROUPS), D ** -0.5),
        'router_g_b': nrm(ks[24], (DEPTH, N_GROUPS), 0.01),
        'router_e_w': nrm(ks[25], (DEPTH, D, N_GROUPS * EXPERTS_PER_GROUP), D ** -0.5),
        'router_e_b': nrm(ks[26], (DEPTH, N_GROUPS * EXPERTS_PER_GROUP), 0.01),
        'exp_w1': nrm(ks[27], (DEPTH, N_GROUPS, EXPERTS_PER_GROUP, D, EXPERT_HIDDEN), D ** -0.5),
        'exp_w3': nrm(ks[28], (DEPTH, N_GROUPS, EXPERTS_PER_GROUP, D, EXPERT_HIDDEN), D ** -0.5),
        'exp_w2': nrm(ks[29], (DEPTH, N_GROUPS, EXPERTS_PER_GROUP, EXPERT_HIDDEN, D), EXPERT_HIDDEN ** -0.5),
        'final_norm_w': 1.0 + nrm(ks[30], (D,), 0.05),
    }


def reference(x, c, ctx, c_ctx, ada_w, ada_b, norm1_w, w_in, hy_conv_w, hy_conv_b, hy_f_w1, hy_f_b1,
              hy_f_w2, hy_f_b2, hy_f_w3, hy_f_b3, hy_f_wout, hy_f_freq, hy_skip, hy_out_norm, ret_gn_w,
              w_out, norm2_w, router_g_w, router_g_b, router_e_w, router_e_b, exp_w1, exp_w3, exp_w2,
              final_norm_w):
    b = x.shape[0]
    for i in range(DEPTH):
        last = i == DEPTH - 1
        lp = {
            'w_in': w_in[i], 'hy_conv_w': hy_conv_w[i], 'hy_conv_b': hy_conv_b[i],
            'hy_f_w1': hy_f_w1[i], 'hy_f_b1': hy_f_b1[i], 'hy_f_w2': hy_f_w2[i], 'hy_f_b2': hy_f_b2[i],
            'hy_f_w3': hy_f_w3[i], 'hy_f_b3': hy_f_b3[i], 'hy_f_wout': hy_f_wout[i], 'hy_f_freq': hy_f_freq[i],
            'hy_skip': hy_skip[i], 'hy_out_norm': hy_out_norm[i], 'ret_gn_w': ret_gn_w[i], 'w_out': w_out[i],
            'router_g_w': router_g_w[i], 'router_g_b': router_g_b[i], 'router_e_w': router_e_w[i],
            'router_e_b': router_e_b[i], 'exp_w1': exp_w1[i], 'exp_w3': exp_w3[i], 'exp_w2': exp_w2[i],
        }
        mod = jax.nn.silu(c) @ ada_w[i] + ada_b[i]
        sh1, sc1, g1, sh2, sc2, g2 = [m[:, None, :] for m in jnp.split(mod, N_MOD, axis=-1)]
        mod_c = jax.nn.silu(c_ctx) @ ada_w[i] + ada_b[i]
        csh1, csc1, cg1, csh2, csc2, cg2 = jnp.split(mod_c, N_MOD, axis=-1)

        hc = modulate(rmsnorm(ctx, norm1_w[i]), csh1, csc1)
        if last:
            s_f, s_b = context_states(hc, lp)
        else:
            zero_state = jnp.zeros((b, RET_HEADS, RET_HEAD_DIM, RET_HEAD_DIM), F32)
            yc, s_f, s_b = token_mixer(hc, lp, zero_state, zero_state, False)
            ctx = ctx + cg1 * yc
            ctx = ctx + cg2 * hier_moe(modulate(rmsnorm(ctx, norm2_w[i]), csh2, csc2), lp)

        h = modulate(rmsnorm(x, norm1_w[i]), sh1, sc1)
        y, _, _ = token_mixer(h, lp, s_f, s_b, True)
        x = x + g1 * y
        x = x + g2 * hier_moe(modulate(rmsnorm(x, norm2_w[i]), sh2, sc2), lp)
    return rmsnorm(x, final_norm_w)
```

```python
import functools
import math

import jax
import jax.numpy as jnp
from jax import lax
from jax.experimental import pallas as pl
from jax.experimental.pallas import tpu as pltpu

F32 = jnp.float32
BF16 = jnp.bfloat16
HIGHEST = lax.Precision.HIGHEST

D_MODEL = 1024
BATCH = 8
SEQ = 2048
CTX_LEN = 256
GRID_W = 64
HY_WIDTH = 512
RET_WIDTH = 512
RET_HEADS = 4
RET_HEAD_DIM = 128
PROJ_COLS = 3 * HY_WIDTH + 4 * RET_WIDTH
RET_COL0 = 3 * HY_WIDTH
RET_DECAY_OFFSET_F = 5.0
RET_DECAY_OFFSET_B = 5.5
ROPE_AXIS_DIM = RET_HEAD_DIM // 4
ROPE_BASE = 10000.0
HY_EMB_DIM = 33
HY_FILTER_HIDDEN = 64
HY_DECAY_TARGET = 1e-2
HY_FAST_PCT = 0.3
HY_SLOW_PCT = 1.5
HY_DECAY_SHIFT = 0.05
N_GROUPS = 4
EXPERTS_PER_GROUP = 4
N_EXPERTS = N_GROUPS * EXPERTS_PER_GROUP
EXPERT_HIDDEN = D_MODEL // 2
N_MOD = 6
EPS = 1e-6

N_FFT = 2 * SEQ
RET_BLOCK = 256
LANES = 128
MOD_ROWS = 16
TOKENS = BATCH * SEQ
VMEM_LIMIT = 56 * 1024 * 1024


def _cparams(sem, vmem=None):
    return pltpu.CompilerParams(dimension_semantics=sem, vmem_limit_bytes=vmem)


def _rms(x, w):
    return x * lax.rsqrt(jnp.mean(x * x, axis=-1, keepdims=True) + EPS) * w


def _silu(x):
    return x * jax.nn.sigmoid(x)


def _ada_kernel(c_ref, w_ref, b_ref, o_ref):
    s = _silu(c_ref[...])
    o_ref[...] = jnp.dot(s, w_ref[...], precision=HIGHEST, preferred_element_type=F32) + b_ref[...]


def _ada_mod(c_rows, ada_w, ada_b):
    ncol = N_MOD * D_MODEL
    return pl.pallas_call(
        _ada_kernel,
        grid=(N_MOD,),
        in_specs=[
            pl.BlockSpec((MOD_ROWS, D_MODEL), lambda j: (0, 0)),
            pl.BlockSpec((D_MODEL, D_MODEL), lambda j: (0, j)),
            pl.BlockSpec((1, D_MODEL), lambda j: (0, j)),
        ],
        out_specs=pl.BlockSpec((MOD_ROWS, D_MODEL), lambda j: (0, j)),
        out_shape=jax.ShapeDtypeStruct((MOD_ROWS, ncol), F32),
        compiler_params=_cparams(("arbitrary",)),
        name="ada_mod",
    )(c_rows, ada_w, ada_b.reshape(1, ncol))


def _ctx_kernel(ctx_ref, nw_ref, sh_ref, sc_ref, wkv_ref, wf_ref, wb_ref, sf_ref, sb_ref):
    h = _rms(ctx_ref[0], nw_ref[...]) * (1.0 + sc_ref[0]) + sh_ref[0]
    kv = jnp.dot(h.astype(BF16), wkv_ref[...], preferred_element_type=F32)
    scale = RET_HEAD_DIM ** -0.5
    for hd in range(RET_HEADS):
        k = kv[:, hd * RET_HEAD_DIM:(hd + 1) * RET_HEAD_DIM] * scale
        v = kv[:, RET_WIDTH + hd * RET_HEAD_DIM:RET_WIDTH + (hd + 1) * RET_HEAD_DIM].astype(BF16)
        kf = (k * wf_ref[hd]).T.astype(BF16)
        kb = (k * wb_ref[hd]).T.astype(BF16)
        sf_ref[0, hd] = jnp.dot(kf, v, preferred_element_type=F32)
        sb_ref[0, hd] = jnp.dot(kb, v, preferred_element_type=F32)


def _context_states(ctx, norm_w, mod3, wkv, wf_tab, wb_tab):
    ctx_row = BATCH * N_MOD
    st = jax.ShapeDtypeStruct((BATCH, RET_HEADS, RET_HEAD_DIM, RET_HEAD_DIM), F32)
    st_spec = pl.BlockSpec((1, RET_HEADS, RET_HEAD_DIM, RET_HEAD_DIM), lambda b: (b, 0, 0, 0))
    tab_spec = pl.BlockSpec((RET_HEADS, CTX_LEN, LANES), lambda b: (0, 0, 0))
    return pl.pallas_call(
        _ctx_kernel,
        grid=(BATCH,),
        in_specs=[
            pl.BlockSpec((1, CTX_LEN, D_MODEL), lambda b: (b, 0, 0)),
            pl.BlockSpec((1, D_MODEL), lambda b: (0, 0)),
            pl.BlockSpec((1, 1, D_MODEL), lambda b: (ctx_row + 0, 0, 0)),
            pl.BlockSpec((1, 1, D_MODEL), lambda b: (ctx_row + 1, 0, 0)),
            pl.BlockSpec((D_MODEL, 2 * RET_WIDTH), lambda b: (0, 0)),
            tab_spec, tab_spec,
        ],
        out_specs=[st_spec, st_spec],
        out_shape=[st, st],
        compiler_params=_cparams(("arbitrary",)),
        name="ctx_states",
    )(ctx, norm_w, mod3, mod3, wkv, wf_tab, wb_tab)


PROJ_TM = 512
PROJ_CHUNK = 512


def _proj_kernel(x_ref, nw_ref, sh_ref, sc_ref, w_ref, o_ref):
    h = _rms(x_ref[...], nw_ref[...]) * (1.0 + sc_ref[0]) + sh_ref[0]
    hb = h.astype(BF16)
    for j in range(PROJ_COLS // PROJ_CHUNK):
        cs = slice(j * PROJ_CHUNK, (j + 1) * PROJ_CHUNK)
        o_ref[:, cs] = jnp.dot(hb, w_ref[:, cs], preferred_element_type=F32).astype(BF16)


def _in_proj(x2, norm_w, mod3, w_in_bf):
    tiles_per_seq = SEQ // PROJ_TM
    return pl.pallas_call(
        _proj_kernel,
        grid=(TOKENS // PROJ_TM,),
        in_specs=[
            pl.BlockSpec((PROJ_TM, D_MODEL), lambda i: (i, 0)),
            pl.BlockSpec((1, D_MODEL), lambda i: (0, 0)),
            pl.BlockSpec((1, 1, D_MODEL), lambda i: ((i // tiles_per_seq) * N_MOD + 0, 0, 0)),
            pl.BlockSpec((1, 1, D_MODEL), lambda i: ((i // tiles_per_seq) * N_MOD + 1, 0, 0)),
            pl.BlockSpec((D_MODEL, PROJ_COLS), lambda i: (0, 0)),
        ],
        out_specs=pl.BlockSpec((PROJ_TM, PROJ_COLS), lambda i: (i, 0)),
        out_shape=jax.ShapeDtypeStruct((TOKENS, PROJ_COLS), BF16),
        compiler_params=_cparams(("arbitrary",), VMEM_LIMIT),
        name="in_proj",
    )(x2, norm_w, mod3, mod3, w_in_bf)


def _split_bf16(a):
    hi = a.astype(BF16)
    lo = (a - hi.astype(F32)).astype(BF16)
    return hi, lo


FILT_RB = 256


def _filter_taps_kernel(w1_ref, b1_ref, w2_ref, b2_ref, w3_ref, b3_ref, wo_ref, fr_ref,
                        shi_ref, slo_ref, dhi_ref, dlo_ref, hn_ref):
    n_tok = SEQ
    n_bands = (HY_EMB_DIM - 1) // 2
    r0 = pl.program_id(0) * FILT_RB
    row = (lax.broadcasted_iota(jnp.int32, (FILT_RB, LANES), 0) + r0).astype(F32)
    lane = lax.broadcasted_iota(jnp.int32, (FILT_RB, LANES), 1)
    t = row * (1.0 / (n_tok - 1))
    band_idx = jnp.where(lane <= n_bands, lane - 1, lane - 1 - n_bands).astype(F32)
    band = 1e-4 + band_idx * ((n_bands - 1 - 1e-4) / (n_bands - 1))
    ang = (2.0 * math.pi / n_tok) * row * band
    z = jnp.where(lane == 0, t,
                  jnp.where(lane <= n_bands, jnp.cos(ang),
                            jnp.where(lane <= 2 * n_bands, -jnp.sin(ang), 0.0)))
    fr = fr_ref[...]

    def lin(a, w_ref_, b_ref_):
        return jnp.dot(a, w_ref_[...], precision=HIGHEST, preferred_element_type=F32) + b_ref_[...]

    h = jnp.sin(fr * lin(z, w1_ref, b1_ref))
    h = jnp.sin(fr * lin(h, w2_ref, b2_ref))
    h = jnp.sin(fr * lin(h, w3_ref, b3_ref))
    h = jnp.dot(h, wo_ref[...], precision=HIGHEST, preferred_element_type=F32)

    max_decay = math.log(HY_DECAY_TARGET) / HY_FAST_PCT
    min_decay = math.log(HY_DECAY_TARGET) / HY_SLOW_PCT
    ch = lax.broadcasted_iota(jnp.int32, (FILT_RB, HY_WIDTH), 1).astype(F32)
    deltas = jnp.abs(min_decay + ch * ((max_decay - min_decay) / (HY_WIDTH - 1)))
    rowc = lax.broadcasted_iota(jnp.int32, (FILT_RB, HY_WIDTH), 0) + r0
    tc = rowc.astype(F32) * (1.0 / (n_tok - 1))
    window = jnp.exp(-tc * deltas) + HY_DECAY_SHIFT
    h_fwd = h[:, :HY_WIDTH] * window
    h_bwd = jnp.where(rowc == 0, 0.0, h[:, HY_WIDTH:] * window)

    g_sum = h_fwd + h_bwd
    g_dif = h_fwd - h_bwd
    shi_ref[...], slo_ref[...] = _split_bf16(g_sum)
    dhi_ref[...], dlo_ref[...] = _split_bf16(g_dif)
    sgn = (1 - 2 * (rowc & 1)).astype(F32)

    @pl.when(pl.program_id(0) == 0)
    def _():
        hn_ref[...] = jnp.zeros_like(hn_ref)

    hn_ref[...] += jnp.sum(g_sum * sgn, axis=0, keepdims=True)


def _filter_spectrum_kernel(cm_ref, sm_ref, shi_ref, slo_ref, dhi_ref, dlo_ref, hr_ref, hi_ref):
    cm = cm_ref[...]
    sm = sm_ref[...]
    hr_ref[...] = (jnp.dot(cm, shi_ref[...], preferred_element_type=F32)
                   + jnp.dot(cm, slo_ref[...], preferred_element_type=F32))
    hi_ref[...] = -(jnp.dot(sm, dhi_ref[...], preferred_element_type=F32)
                    + jnp.dot(sm, dlo_ref[...], preferred_element_type=F32))


def _hyena_filter_spectrum(w1p, b1, w2, b2, w3, b3, wout, freq, cm, sm):
    def whole(a):
        return pl.BlockSpec(a.shape, lambda i: (0,) * a.ndim)

    tap_spec = pl.BlockSpec((FILT_RB, HY_WIDTH), lambda i: (i, 0))
    tap_shape = jax.ShapeDtypeStruct((SEQ, HY_WIDTH), BF16)
    params = (w1p, b1, w2, b2, w3, b3, wout, freq)
    s_hi, s_lo, d_hi, d_lo, hn = pl.pallas_call(
        _filter_taps_kernel,
        grid=(SEQ // FILT_RB,),
        in_specs=[whole(a) for a in params],
        out_specs=[tap_spec] * 4 + [pl.BlockSpec((1, HY_WIDTH), lambda i: (0, 0))],
        out_shape=[tap_shape] * 4 + [jax.ShapeDtypeStruct((1, HY_WIDTH), F32)],
        compiler_params=_cparams(("arbitrary",)),
        name="hyena_filter_taps",
    )(*params)
    full = pl.BlockSpec((SEQ, HY_WIDTH), lambda i: (0, 0))
    tab = pl.BlockSpec((FILT_RB, SEQ), lambda i: (i, 0))
    spec_out = pl.BlockSpec((FILT_RB, HY_WIDTH), lambda i: (i, 0))
    hr, hi = pl.pallas_call(
        _filter_spectrum_kernel,
        grid=(SEQ // FILT_RB,),
        in_specs=[tab, tab, full, full, full, full],
        out_specs=[spec_out, spec_out],
        out_shape=[jax.ShapeDtypeStruct((SEQ, HY_WIDTH), F32)] * 2,
        compiler_params=_cparams(("arbitrary",), VMEM_LIMIT),
        name="hyena_filter_spectrum",
    )(cm, sm, s_hi, s_lo, d_hi, d_lo)
    return hr, hi, hn


HY_CB = 256


def _conv3(p, w_ref, b_ref):
    n = p.shape[0]
    row = lax.broadcasted_iota(jnp.int32, p.shape, 0)
    prev = jnp.where(row == 0, 0.0, pltpu.roll(p, 1, 0))
    nxt = jnp.where(row == n - 1, 0.0, pltpu.roll(p, n - 1, 0))
    return prev * w_ref[0:1, :] + p * w_ref[1:2, :] + nxt * w_ref[2:3, :] + b_ref[...]


HY_RB = 256


def _hyena_kernel(x0_ref, x1_ref, v_ref, w0_ref, w1_ref, w2_ref, b0_ref, b1_ref, b2_ref,
                  cm_ref, sm_ref, hr_ref, hi_ref, hn_ref, skip_ref, o_ref,
                  u_scr, ub_scr, x0_scr, za_scr, zb_scr):
    x1c = _conv3(x1_ref[0].astype(F32), w1_ref, b1_ref)
    vc = _conv3(v_ref[0].astype(F32), w2_ref, b2_ref)
    u = vc * x1c
    u_scr[...] = u
    ub_scr[...] = u.astype(BF16)
    x0_scr[...] = _conv3(x0_ref[0].astype(F32), w0_ref, b0_ref)
    row = lax.broadcasted_iota(jnp.int32, u.shape, 0)
    xn = jnp.sum(u * (1 - 2 * (row & 1)).astype(F32), axis=0, keepdims=True)
    zn = xn * hn_ref[...] * (1.0 / N_FFT)

    def rows(r):
        return pl.ds(pl.multiple_of(r * HY_RB, HY_RB), HY_RB)

    def fwd(r, carry):
        rs = rows(r)
        ub = ub_scr[...]
        xr = jnp.dot(cm_ref[rs, :], ub, preferred_element_type=F32)
        xs = jnp.dot(sm_ref[rs, :], ub, preferred_element_type=F32)
        hr = hr_ref[rs, :]
        hi = hi_ref[rs, :]
        k = lax.broadcasted_iota(jnp.int32, xr.shape, 0) + r * HY_RB
        wk = jnp.where(k == 0, 1.0 / N_FFT, 2.0 / N_FFT)
        za_scr[rs, :] = ((xr * hr + xs * hi) * wk).astype(BF16)
        zb_scr[rs, :] = ((xs * hr - xr * hi) * wk).astype(BF16)
        return carry

    lax.fori_loop(0, SEQ // HY_RB, fwd, 0)

    def inv(r, carry):
        rs = rows(r)
        y = (jnp.dot(cm_ref[rs, :], za_scr[...], preferred_element_type=F32)
             + jnp.dot(sm_ref[rs, :], zb_scr[...], preferred_element_type=F32))
        tpos = lax.broadcasted_iota(jnp.int32, y.shape, 0)
        y = y + (1 - 2 * (tpos & 1)).astype(F32) * zn
        y = y + u_scr[rs, :] * skip_ref[...]
        o_ref[0, rs, :] = x0_scr[rs, :] * y
        return carry

    lax.fori_loop(0, SEQ // HY_RB, inv, 0)


def _hyena_mix(proj3, conv_w, conv_b, cm, sm, hr, hi, hn, skip):
    ncb = HY_WIDTH // HY_CB

    def part_spec(part):
        return pl.BlockSpec((1, SEQ, HY_CB), lambda b, c: (b, 0, part * ncb + c))

    def w_spec(part):
        return pl.BlockSpec((3, HY_CB), lambda b, c: (0, part * ncb + c))

    def b_spec(part):
        return pl.BlockSpec((1, HY_CB), lambda b, c: (0, part * ncb + c))

    tab_spec = pl.BlockSpec((SEQ, SEQ), lambda b, c: (0, 0), pipeline_mode=pl.Buffered(1))
    return pl.pallas_call(
        _hyena_kernel,
        grid=(BATCH, ncb),
        in_specs=[part_spec(0), part_spec(1), part_spec(2),
                  w_spec(0), w_spec(1), w_spec(2), b_spec(0), b_spec(1), b_spec(2),
                  tab_spec, tab_spec,
                  pl.BlockSpec((SEQ, HY_CB), lambda b, c: (0, c)),
                  pl.BlockSpec((SEQ, HY_CB), lambda b, c: (0, c)),
                  pl.BlockSpec((1, HY_CB), lambda b, c: (0, c)),
                  pl.BlockSpec((1, HY_CB), lambda b, c: (0, c))],
        out_specs=pl.BlockSpec((1, SEQ, HY_CB), lambda b, c: (b, 0, c)),
        out_shape=jax.ShapeDtypeStruct((BATCH, SEQ, HY_WIDTH), F32),
        scratch_shapes=[pltpu.VMEM((SEQ, HY_CB), F32), pltpu.VMEM((SEQ, HY_CB), BF16),
                        pltpu.VMEM((SEQ, HY_CB), F32), pltpu.VMEM((SEQ, HY_CB), BF16),
                        pltpu.VMEM((SEQ, HY_CB), BF16)],
        compiler_params=_cparams(("arbitrary", "arbitrary"), VMEM_LIMIT),
        name="hyena_mix",
    )(proj3, proj3, proj3, conv_w, conv_w, conv_w, conv_b, conv_b, conv_b,
      cm, sm, hr, hi, hn, skip)


def _rope(t, cos_ref, sina_ref, sinb_ref):
    half = ROPE_AXIS_DIM // 2
    return (t * cos_ref[...]
            + pltpu.roll(t, LANES - half, 1) * sina_ref[...]
            + pltpu.roll(t, half, 1) * sinb_ref[...])


def _nt_dot(a, b):
    return lax.dot_general(a, b, (((1,), (1,)), ((), ())), preferred_element_type=F32)


def _ret_kernel(q_ref, k_ref, v_ref, g_ref, cos_ref, sina_ref, sinb_ref, m_ref,
                xif_ref, xib_ref, zf_ref, zb_ref, gam_ref, s0f_ref, s0b_ref, gnw_ref, o_ref):
    nblk = SEQ // RET_BLOCK
    q = _rope(q_ref[0].astype(F32), cos_ref, sina_ref, sinb_ref)
    k = _rope(k_ref[0].astype(F32) * (RET_HEAD_DIM ** -0.5), cos_ref, sina_ref, sinb_ref)
    qb = q.astype(BF16)
    vb = v_ref[0]
    zf = zf_ref[0]
    zb = zb_ref[0]
    gf = gam_ref[0, 0:1, :]
    gb = gam_ref[0, 1:2, :]

    def blk(a, c):
        return a[c * RET_BLOCK:(c + 1) * RET_BLOCK]

    a_f, a_b = [], []
    for c in range(nblk):
        kc = blk(k, c)
        vc = blk(vb, c)
        a_f.append(jnp.dot((kc * zf).T.astype(BF16), vc, preferred_element_type=F32))
        a_b.append(jnp.dot((kc * zb).T.astype(BF16), vc, preferred_element_type=F32))
    s_f = [None] * nblk
    s_b = [None] * nblk
    s_f[0] = s0f_ref[0, 0]
    for c in range(1, nblk):
        s_f[c] = s_f[c - 1] * gf + a_f[c - 1]
    s_b[nblk - 1] = s0b_ref[0, 0]
    for c in range(nblk - 2, -1, -1):
        s_b[c] = s_b[c + 1] * gb + a_b[c + 1]

    m = m_ref[0]
    xif = xif_ref[0]
    xib = xib_ref[0]
    gnw = gnw_ref[...]
    for c in range(nblk):
        qc = blk(qb, c)
        kc = blk(k, c).astype(BF16)
        vc = blk(vb, c)
        p = (_nt_dot(qc, kc) * m).astype(BF16)
        o = jnp.dot(p, vc, preferred_element_type=F32)
        o = o + jnp.dot(qc, s_f[c].astype(BF16), preferred_element_type=F32) * xif
        o = o + jnp.dot(qc, s_b[c].astype(BF16), preferred_element_type=F32) * xib
        mu = jnp.mean(o, axis=-1, keepdims=True)
        oc = o - mu
        var = jnp.mean(oc * oc, axis=-1, keepdims=True)
        on = oc * lax.rsqrt(var + EPS) * gnw
        gc = g_ref[0, c * RET_BLOCK:(c + 1) * RET_BLOCK, :].astype(F32)
        o_ref[0, c * RET_BLOCK:(c + 1) * RET_BLOCK, :] = (_silu(gc) * on).astype(BF16)


def _retention(proj3, tabs, s0f, s0b, gn_w):
    cos_t, sina_t, sinb_t, m_t, xif_t, xib_t, zf_t, zb_t, gam_t = tabs
    c0 = RET_COL0 // RET_HEAD_DIM

    def qkvg(part):
        off = c0 + part * RET_HEADS
        return pl.BlockSpec((1, SEQ, RET_HEAD_DIM), lambda b, h: (b, 0, off + h))

    rope_spec = pl.BlockSpec((SEQ, RET_HEAD_DIM), lambda b, h: (0, 0))
    vec_spec = pl.BlockSpec((1, RET_BLOCK, LANES), lambda b, h: (h, 0, 0))
    st_spec = pl.BlockSpec((1, 1, RET_HEAD_DIM, RET_HEAD_DIM), lambda b, h: (b, h, 0, 0))
    return pl.pallas_call(
        _ret_kernel,
        grid=(BATCH, RET_HEADS),
        in_specs=[qkvg(0), qkvg(1), qkvg(2), qkvg(3),
                  rope_spec, rope_spec, rope_spec,
                  pl.BlockSpec((1, RET_BLOCK, RET_BLOCK), lambda b, h: (h, 0, 0)),
                  vec_spec, vec_spec, vec_spec, vec_spec,
                  pl.BlockSpec((1, 8, LANES), lambda b, h: (h, 0, 0)),
                  st_spec, st_spec,
                  pl.BlockSpec((1, RET_HEAD_DIM), lambda b, h: (0, h))],
        out_specs=pl.BlockSpec((1, SEQ, RET_HEAD_DIM), lambda b, h: (b, 0, h)),
        out_shape=jax.ShapeDtypeStruct((BATCH, SEQ, RET_WIDTH), BF16),
        compiler_params=_cparams(("arbitrary", "arbitrary"), VMEM_LIMIT),
        name="retention",
    )(proj3, proj3, proj3, proj3, cos_t, sina_t, sinb_t, m_t, xif_t, xib_t, zf_t, zb_t, gam_t,
      s0f, s0b, gn_w)


MIX_TM = 512


def _mix_out_kernel(x_ref, yh_ref, yr_ref, hnw_ref, wo_ref, g1_ref, n2w_ref, sh2_ref, sc2_ref,
                    rw_ref, rb_ref, x1_ref, t_ref, comb_ref):
    yh = _rms(yh_ref[...], hnw_ref[...]).astype(BF16)
    y = (jnp.dot(yh, wo_ref[0:HY_WIDTH, :], preferred_element_type=F32)
         + jnp.dot(yr_ref[...], wo_ref[HY_WIDTH:, :], preferred_element_type=F32))
    x1 = x_ref[...] + g1_ref[0] * y
    x1_ref[...] = x1
    t = _rms(x1, n2w_ref[...]) * (1.0 + sc2_ref[0]) + sh2_ref[0]
    t_ref[...] = t.astype(BF16)

    logits = jnp.dot(t, rw_ref[...], precision=HIGHEST, preferred_element_type=F32) + rb_ref[...]
    lane = lax.broadcasted_iota(jnp.int32, logits.shape, 1).astype(F32)
    neg = -jnp.inf
    gmask = (lane >= N_EXPERTS) & (lane < N_EXPERTS + N_GROUPS)
    lg = jnp.where(gmask, logits, neg)
    mg = jnp.max(lg, axis=-1, keepdims=True)
    p_sel = 1.0 / jnp.sum(jnp.exp(lg - mg), axis=-1, keepdims=True)
    gi = jnp.min(jnp.where(lg == mg, lane, float(LANES)), axis=-1, keepdims=True) - N_EXPERTS
    emask = (lane >= gi * EXPERTS_PER_GROUP) & (lane < (gi + 1) * EXPERTS_PER_GROUP)
    le = jnp.where(emask, logits, neg)
    me = jnp.max(le, axis=-1, keepdims=True)
    ee = jnp.exp(le - me)
    pe = ee / jnp.sum(ee, axis=-1, keepdims=True)
    pe = jnp.where(emask, pe, -1.0)
    v1 = jnp.max(pe, axis=-1, keepdims=True)
    i1 = jnp.min(jnp.where(pe == v1, lane, float(LANES)), axis=-1, keepdims=True)
    pe2 = jnp.where(lane == i1, -1.0, pe)
    v2 = jnp.max(pe2, axis=-1, keepdims=True)
    i2 = jnp.min(jnp.where(pe2 == v2, lane, float(LANES)), axis=-1, keepdims=True)
    den = v1 + v2
    comb_ref[...] = p_sel * (jnp.where(lane == i1, v1 / den, 0.0) + jnp.where(lane == i2, v2 / den, 0.0))


def _mix_out(x2, y_hy, y_ret, hy_norm_w, w_out_bf, mod3, norm2_w, rw, rb):
    tiles_per_seq = SEQ // MIX_TM

    def mod_spec(j):
        return pl.BlockSpec((1, 1, D_MODEL), lambda i: ((i // tiles_per_seq) * N_MOD + j, 0, 0))

    row_spec = pl.BlockSpec((MIX_TM, D_MODEL), lambda i: (i, 0))
    return pl.pallas_call(
        _mix_out_kernel,
        grid=(TOKENS // MIX_TM,),
        in_specs=[row_spec,
                  pl.BlockSpec((MIX_TM, HY_WIDTH), lambda i: (i, 0)),
                  pl.BlockSpec((MIX_TM, RET_WIDTH), lambda i: (i, 0)),
                  pl.BlockSpec((1, HY_WIDTH), lambda i: (0, 0)),
                  pl.BlockSpec((D_MODEL, D_MODEL), lambda i: (0, 0)),
                  mod_spec(2),
                  pl.BlockSpec((1, D_MODEL), lambda i: (0, 0)),
                  mod_spec(3), mod_spec(4),
                  pl.BlockSpec((D_MODEL, LANES), lambda i: (0, 0)),
                  pl.BlockSpec((1, LANES), lambda i: (0, 0))],
        out_specs=[row_spec, row_spec, pl.BlockSpec((MIX_TM, LANES), lambda i: (i, 0))],
        out_shape=[jax.ShapeDtypeStruct((TOKENS, D_MODEL), F32),
                   jax.ShapeDtypeStruct((TOKENS, D_MODEL), BF16),
                   jax.ShapeDtypeStruct((TOKENS, LANES), F32)],
        compiler_params=_cparams(("arbitrary",), VMEM_LIMIT),
        name="mix_out",
    )(x2, y_hy, y_ret, hy_norm_w, w_out_bf, mod3, norm2_w, mod3, mod3, rw, rb)


MOE_TM = 512


def _moe_kernel(t_ref, comb_ref, w1_ref, w3_ref, w2_ref, x1_ref, g2_ref, fnw_ref, o_ref, acc_ref):
    e = pl.program_id(1)

    @pl.when(e == 0)
    def _():
        acc_ref[...] = jnp.zeros_like(acc_ref)

    t = t_ref[...]
    a = jnp.dot(t, w1_ref[0], preferred_element_type=F32)
    u = jnp.dot(t, w3_ref[0], preferred_element_type=F32)
    comb = comb_ref[...]
    lane = lax.broadcasted_iota(jnp.int32, comb.shape, 1)
    ce = jnp.sum(jnp.where(lane == e, comb, 0.0), axis=-1, keepdims=True)
    hid = (_silu(a) * u * ce).astype(BF16)
    acc_ref[...] += jnp.dot(hid, w2_ref[0], preferred_element_type=F32)

    @pl.when(e == N_EXPERTS - 1)
    def _():
        o_ref[...] = _rms(x1_ref[...] + g2_ref[0] * acc_ref[...], fnw_ref[...])


def _moe_dense(t_bf, comb, w1, w3, w2, x1, mod3, final_w):
    tiles_per_seq = SEQ // MOE_TM
    row_spec = pl.BlockSpec((MOE_TM, D_MODEL), lambda i, e: (i, 0))
    return pl.pallas_call(
        _moe_kernel,
        grid=(TOKENS // MOE_TM, N_EXPERTS),
        in_specs=[row_spec,
                  pl.BlockSpec((MOE_TM, LANES), lambda i, e: (i, 0)),
                  pl.BlockSpec((1, D_MODEL, EXPERT_HIDDEN), lambda i, e: (e, 0, 0)),
                  pl.BlockSpec((1, D_MODEL, EXPERT_HIDDEN), lambda i, e: (e, 0, 0)),
                  pl.BlockSpec((1, EXPERT_HIDDEN, D_MODEL), lambda i, e: (e, 0, 0)),
                  row_spec,
                  pl.BlockSpec((1, 1, D_MODEL), lambda i, e: ((i // tiles_per_seq) * N_MOD + 5, 0, 0)),
                  pl.BlockSpec((1, D_MODEL), lambda i, e: (0, 0))],
        out_specs=row_spec,
        out_shape=jax.ShapeDtypeStruct((TOKENS, D_MODEL), F32),
        scratch_shapes=[pltpu.VMEM((MOE_TM, D_MODEL), F32)],
        compiler_params=_cparams(("arbitrary", "arbitrary"), VMEM_LIMIT),
        name="moe_dense",
    )(t_bf, comb, w1, w3, w2, x1, mod3, final_w)


def _dft_tables():
    k = lax.broadcasted_iota(jnp.int32, (SEQ, SEQ), 0)
    t = lax.broadcasted_iota(jnp.int32, (SEQ, SEQ), 1)
    ang = ((k * t) & (N_FFT - 1)).astype(F32) * (2.0 * math.pi / N_FFT)
    return jnp.cos(ang).astype(BF16), jnp.sin(ang).astype(BF16)


def _log_gammas(offset):
    return jnp.log1p(-jnp.exp2(-(offset + jnp.arange(RET_HEADS, dtype=F32))))


def _lane_bcast(a):
    return jnp.broadcast_to(a[..., None], a.shape + (LANES,))


def _retention_tables():
    n = jnp.arange(SEQ)
    rows = (n // GRID_W).astype(F32)
    cols = (n % GRID_W).astype(F32)
    n_freq = ROPE_AXIS_DIM // 2
    freqs = ROPE_BASE ** (-jnp.arange(n_freq, dtype=F32) / n_freq)
    ar = rows[:, None] * freqs[None, :]
    ac = cols[:, None] * freqs[None, :]
    zeros = jnp.zeros((SEQ, n_freq), F32)
    rest = RET_HEAD_DIM - 2 * ROPE_AXIS_DIM
    cos_t = jnp.concatenate([jnp.cos(ar), jnp.cos(ar), jnp.cos(ac), jnp.cos(ac),
                             jnp.ones((SEQ, rest), F32)], axis=-1)
    sina_t = jnp.concatenate([-jnp.sin(ar), zeros, -jnp.sin(ac), zeros,
                              jnp.zeros((SEQ, rest), F32)], axis=-1)
    sinb_t = jnp.concatenate([zeros, jnp.sin(ar), zeros, jnp.sin(ac),
                              jnp.zeros((SEQ, rest), F32)], axis=-1)
    lgf = _log_gammas(RET_DECAY_OFFSET_F)
    lgb = _log_gammas(RET_DECAY_OFFSET_B)
    i = jnp.arange(RET_BLOCK, dtype=F32)
    diff = i[:, None] - i[None, :]
    m_t = jnp.where(diff > 0, jnp.exp(jnp.maximum(diff, 0.0)[None] * lgf[:, None, None]),
                    jnp.where(diff < 0, jnp.exp(jnp.maximum(-diff, 0.0)[None] * lgb[:, None, None]), 2.0))
    xif = _lane_bcast(jnp.exp((i + 1.0)[None, :] * lgf[:, None]))
    xib = _lane_bcast(jnp.exp((RET_BLOCK - i)[None, :] * lgb[:, None]))
    zf = _lane_bcast(jnp.exp((RET_BLOCK - 1.0 - i)[None, :] * lgf[:, None]))
    zb = _lane_bcast(jnp.exp(i[None, :] * lgb[:, None]))
    gam = jnp.zeros((RET_HEADS, 8, LANES), F32)
    gam = gam.at[:, 0, :].set(jnp.exp(RET_BLOCK * lgf)[:, None])
    gam = gam.at[:, 1, :].set(jnp.exp(RET_BLOCK * lgb)[:, None])
    j = jnp.arange(CTX_LEN, dtype=F32)
    wf = _lane_bcast(jnp.exp((CTX_LEN - 1.0 - j)[None, :] * lgf[:, None]))
    wb = _lane_bcast(jnp.exp(j[None, :] * lgb[:, None]))
    return (cos_t, sina_t, sinb_t, m_t, xif, xib, zf, zb, gam), wf, wb


def kernel(x, c, ctx, c_ctx, ada_w, ada_b, norm1_w, w_in, hy_conv_w, hy_conv_b, hy_f_w1, hy_f_b1,
           hy_f_w2, hy_f_b2, hy_f_w3, hy_f_b3, hy_f_wout, hy_f_freq, hy_skip, hy_out_norm, ret_gn_w,
           w_out, norm2_w, router_g_w, router_g_b, router_e_w, router_e_b, exp_w1, exp_w3, exp_w2,
           final_norm_w):
    x2 = x.reshape(TOKENS, D_MODEL)

    c_rows = jnp.concatenate([c, c_ctx[None, :], jnp.zeros((MOD_ROWS - BATCH - 1, D_MODEL), F32)], axis=0)
    mod = _ada_mod(c_rows, ada_w[0], ada_b[0])
    mod3 = mod.reshape(MOD_ROWS * N_MOD, 1, D_MODEL)

    ret_tabs, wf_tab, wb_tab = _retention_tables()
    cm, sm = _dft_tables()

    w_in_bf = w_in[0].astype(BF16)
    k0 = RET_COL0 + RET_WIDTH
    s0f, s0b = _context_states(ctx, norm1_w, mod3, w_in_bf[:, k0:k0 + 2 * RET_WIDTH], wf_tab, wb_tab)

    proj = _in_proj(x2, norm1_w, mod3, w_in_bf)
    proj3 = proj.reshape(BATCH, SEQ, PROJ_COLS)

    w1p = jnp.zeros((LANES, HY_FILTER_HIDDEN), F32).at[:HY_EMB_DIM].set(hy_f_w1[0])
    hr, hi, hn = _hyena_filter_spectrum(w1p, hy_f_b1, hy_f_w2[0], hy_f_b2, hy_f_w3[0], hy_f_b3,
                                        hy_f_wout[0], hy_f_freq, cm, sm)
    y_hy = _hyena_mix(proj3, hy_conv_w[0], hy_conv_b, cm, sm, hr, hi, hn, hy_skip)
    y_ret = _retention(proj3, ret_tabs, s0f, s0b, ret_gn_w)

    rw = jnp.zeros((D_MODEL, LANES), F32)
    rw = rw.at[:, :N_EXPERTS].set(router_e_w[0]).at[:, N_EXPERTS:N_EXPERTS + N_GROUPS].set(router_g_w[0])
    rb = jnp.zeros((1, LANES), F32)
    rb = rb.at[0, :N_EXPERTS].set(router_e_b[0]).at[0, N_EXPERTS:N_EXPERTS + N_GROUPS].set(router_g_b[0])
    x1, t_bf, comb = _mix_out(x2, y_hy.reshape(TOKENS, HY_WIDTH), y_ret.reshape(TOKENS, RET_WIDTH),
                              hy_out_norm, w_out[0].astype(BF16), mod3, norm2_w, rw, rb)

    w1 = exp_w1[0].reshape(N_EXPERTS, D_MODEL, EXPERT_HIDDEN).astype(BF16)
    w3 = exp_w3[0].reshape(N_EXPERTS, D_MODEL, EXPERT_HIDDEN).astype(BF16)
    w2 = exp_w2[0].reshape(N_EXPERTS, EXPERT_HIDDEN, D_MODEL).astype(BF16)
    out = _moe_dense(t_bf, comb, w1, w3, w2, x1, mod3, final_norm_w.reshape(1, D_MODEL))
    return out.reshape(BATCH, SEQ, D_MODEL)
```

```python
import functools
import math

import jax
import jax.numpy as jnp
from jax import lax
from jax.experimental import pallas as pl
from jax.experimental.pallas import tpu as pltpu

F32 = jnp.float32
BF16 = jnp.bfloat16
HIGHEST = lax.Precision.HIGHEST

D_MODEL = 1024
BATCH = 8
SEQ = 2048
CTX_LEN = 256
GRID_W = 64
HY_WIDTH = 512
RET_WIDTH = 512
RET_HEADS = 4
RET_HEAD_DIM = 128
PROJ_COLS = 3 * HY_WIDTH + 4 * RET_WIDTH
RET_COL0 = 3 * HY_WIDTH
RET_DECAY_OFFSET_F = 5.0
RET_DECAY_OFFSET_B = 5.5
ROPE_AXIS_DIM = RET_HEAD_DIM // 4
ROPE_BASE = 10000.0
HY_EMB_DIM = 33
HY_FILTER_HIDDEN = 64
HY_DECAY_TARGET = 1e-2
HY_FAST_PCT = 0.3
HY_SLOW_PCT = 1.5
HY_DECAY_SHIFT = 0.05
N_GROUPS = 4
EXPERTS_PER_GROUP = 4
N_EXPERTS = N_GROUPS * EXPERTS_PER_GROUP
EXPERT_HIDDEN = D_MODEL // 2
N_MOD = 6
EPS = 1e-6

N_FFT = 2 * SEQ
RET_BLOCK = 256
LANES = 128
MOD_ROWS = 16
TOKENS = BATCH * SEQ
VMEM_LIMIT = 56 * 1024 * 1024


def _cparams(sem, vmem=None):
    return pltpu.CompilerParams(dimension_semantics=sem, vmem_limit_bytes=vmem)


def _rms(x, w):
    return x * lax.rsqrt(jnp.mean(x * x, axis=-1, keepdims=True) + EPS) * w


def _silu(x):
    return x * jax.nn.sigmoid(x)


def _ada_kernel(c_ref, w_ref, b_ref, o_ref):
    s = _silu(c_ref[...])
    o_ref[...] = jnp.dot(s, w_ref[...], precision=HIGHEST, preferred_element_type=F32) + b_ref[...]


def _ada_mod(c_rows, ada_w, ada_b):
    ncol = N_MOD * D_MODEL
    return pl.pallas_call(
        _ada_kernel,
        grid=(N_MOD,),
        in_specs=[
            pl.BlockSpec((MOD_ROWS, D_MODEL), lambda j: (0, 0)),
            pl.BlockSpec((D_MODEL, D_MODEL), lambda j: (0, j)),
            pl.BlockSpec((1, D_MODEL), lambda j: (0, j)),
        ],
        out_specs=pl.BlockSpec((MOD_ROWS, D_MODEL), lambda j: (0, j)),
        out_shape=jax.ShapeDtypeStruct((MOD_ROWS, ncol), F32),
        compiler_params=_cparams(("arbitrary",)),
        name="ada_mod",
    )(c_rows, ada_w, ada_b.reshape(1, ncol))


def _ctx_kernel(ctx_ref, nw_ref, sh_ref, sc_ref, wkv_ref, wf_ref, wb_ref, sf_ref, sb_ref):
    h = _rms(ctx_ref[0], nw_ref[...]) * (1.0 + sc_ref[0]) + sh_ref[0]
    kv = jnp.dot(h.astype(BF16), wkv_ref[...], preferred_element_type=F32)
    scale = RET_HEAD_DIM ** -0.5
    for hd in range(RET_HEADS):
        k = kv[:, hd * RET_HEAD_DIM:(hd + 1) * RET_HEAD_DIM] * scale
        v = kv[:, RET_WIDTH + hd * RET_HEAD_DIM:RET_WIDTH + (hd + 1) * RET_HEAD_DIM].astype(BF16)
        kf = (k * wf_ref[hd]).T.astype(BF16)
        kb = (k * wb_ref[hd]).T.astype(BF16)
        sf_ref[0, hd] = jnp.dot(kf, v, preferred_element_type=F32)
        sb_ref[0, hd] = jnp.dot(kb, v, preferred_element_type=F32)


def _context_states(ctx, norm_w, mod3, wkv, wf_tab, wb_tab):
    ctx_row = BATCH * N_MOD
    st = jax.ShapeDtypeStruct((BATCH, RET_HEADS, RET_HEAD_DIM, RET_HEAD_DIM), F32)
    st_spec = pl.BlockSpec((1, RET_HEADS, RET_HEAD_DIM, RET_HEAD_DIM), lambda b: (b, 0, 0, 0))
    tab_spec = pl.BlockSpec((RET_HEADS, CTX_LEN, LANES), lambda b: (0, 0, 0))
    return pl.pallas_call(
        _ctx_kernel,
        grid=(BATCH,),
        in_specs=[
            pl.BlockSpec((1, CTX_LEN, D_MODEL), lambda b: (b, 0, 0)),
            pl.BlockSpec((1, D_MODEL), lambda b: (0, 0)),
            pl.BlockSpec((1, 1, D_MODEL), lambda b: (ctx_row + 0, 0, 0)),
            pl.BlockSpec((1, 1, D_MODEL), lambda b: (ctx_row + 1, 0, 0)),
            pl.BlockSpec((D_MODEL, 2 * RET_WIDTH), lambda b: (0, 0)),
            tab_spec, tab_spec,
        ],
        out_specs=[st_spec, st_spec],
        out_shape=[st, st],
        compiler_params=_cparams(("arbitrary",)),
        name="ctx_states",
    )(ctx, norm_w, mod3, mod3, wkv, wf_tab, wb_tab)


PROJ_TM = 512
PROJ_CHUNK = 512


def _proj_kernel(x_ref, nw_ref, sh_ref, sc_ref, w_ref, o_ref):
    h = _rms(x_ref[...], nw_ref[...]) * (1.0 + sc_ref[0]) + sh_ref[0]
    hb = h.astype(BF16)
    for j in range(PROJ_COLS // PROJ_CHUNK):
        cs = slice(j * PROJ_CHUNK, (j + 1) * PROJ_CHUNK)
        o_ref[:, cs] = jnp.dot(hb, w_ref[:, cs], preferred_element_type=F32).astype(BF16)


def _in_proj(x2, norm_w, mod3, w_in_bf):
    tiles_per_seq = SEQ // PROJ_TM
    return pl.pallas_call(
        _proj_kernel,
        grid=(TOKENS // PROJ_TM,),
        in_specs=[
            pl.BlockSpec((PROJ_TM, D_MODEL), lambda i: (i, 0)),
            pl.BlockSpec((1, D_MODEL), lambda i: (0, 0)),
            pl.BlockSpec((1, 1, D_MODEL), lambda i: ((i // tiles_per_seq) * N_MOD + 0, 0, 0)),
            pl.BlockSpec((1, 1, D_MODEL), lambda i: ((i // tiles_per_seq) * N_MOD + 1, 0, 0)),
            pl.BlockSpec((D_MODEL, PROJ_COLS), lambda i: (0, 0)),
        ],
        out_specs=pl.BlockSpec((PROJ_TM, PROJ_COLS), lambda i: (i, 0)),
        out_shape=jax.ShapeDtypeStruct((TOKENS, PROJ_COLS), BF16),
        compiler_params=_cparams(("arbitrary",), VMEM_LIMIT),
        name="in_proj",
    )(x2, norm_w, mod3, mod3, w_in_bf)


def _split_bf16(a):
    hi = a.astype(BF16)
    lo = (a - hi.astype(F32)).astype(BF16)
    return hi, lo


FILT_RB = 256


def _filter_taps_kernel(w1_ref, b1_ref, w2_ref, b2_ref, w3_ref, b3_ref, wo_ref, fr_ref,
                        shi_ref, slo_ref, dhi_ref, dlo_ref, hn_ref):
    n_tok = SEQ
    n_bands = (HY_EMB_DIM - 1) // 2
    r0 = pl.program_id(0) * FILT_RB
    row = (lax.broadcasted_iota(jnp.int32, (FILT_RB, LANES), 0) + r0).astype(F32)
    lane = lax.broadcasted_iota(jnp.int32, (FILT_RB, LANES), 1)
    t = row * (1.0 / (n_tok - 1))
    band_idx = jnp.where(lane <= n_bands, lane - 1, lane - 1 - n_bands).astype(F32)
    band = 1e-4 + band_idx * ((n_bands - 1 - 1e-4) / (n_bands - 1))
    ang = (2.0 * math.pi / n_tok) * row * band
    z = jnp.where(lane == 0, t,
                  jnp.where(lane <= n_bands, jnp.cos(ang),
                            jnp.where(lane <= 2 * n_bands, -jnp.sin(ang), 0.0)))
    fr = fr_ref[...]

    def lin(a, w_ref_, b_ref_):
        return jnp.dot(a, w_ref_[...], precision=HIGHEST, preferred_element_type=F32) + b_ref_[...]

    h = jnp.sin(fr * lin(z, w1_ref, b1_ref))
    h = jnp.sin(fr * lin(h, w2_ref, b2_ref))
    h = jnp.sin(fr * lin(h, w3_ref, b3_ref))
    h = jnp.dot(h, wo_ref[...], precision=HIGHEST, preferred_element_type=F32)

    max_decay = math.log(HY_DECAY_TARGET) / HY_FAST_PCT
    min_decay = math.log(HY_DECAY_TARGET) / HY_SLOW_PCT
    ch = lax.broadcasted_iota(jnp.int32, (FILT_RB, HY_WIDTH), 1).astype(F32)
    deltas = jnp.abs(min_decay + ch * ((max_decay - min_decay) / (HY_WIDTH - 1)))
    rowc = lax.broadcasted_iota(jnp.int32, (FILT_RB, HY_WIDTH), 0) + r0
    tc = rowc.astype(F32) * (1.0 / (n_tok - 1))
    window = jnp.exp(-tc * deltas) + HY_DECAY_SHIFT
    h_fwd = h[:, :HY_WIDTH] * window
    h_bwd = jnp.where(rowc == 0, 0.0, h[:, HY_WIDTH:] * window)

    g_sum = h_fwd + h_bwd
    g_dif = h_fwd - h_bwd
    shi_ref[...], slo_ref[...] = _split_bf16(g_sum)
    dhi_ref[...], dlo_ref[...] = _split_bf16(g_dif)
    sgn = (1 - 2 * (rowc & 1)).astype(F32)

    @pl.when(pl.program_id(0) == 0)
    def _():
        hn_ref[...] = jnp.zeros_like(hn_ref)

    hn_ref[...] += jnp.sum(g_sum * sgn, axis=0, keepdims=True)


def _filter_spectrum_kernel(cm_ref, sm_ref, shi_ref, slo_ref, dhi_ref, dlo_ref, hr_ref, hi_ref):
    cm = cm_ref[...]
    sm = sm_ref[...]
    hr_ref[...] = (jnp.dot(cm, shi_ref[...], preferred_element_type=F32)
                   + jnp.dot(cm, slo_ref[...], preferred_element_type=F32))
    hi_ref[...] = -(jnp.dot(sm, dhi_ref[...], preferred_element_type=F32)
                    + jnp.dot(sm, dlo_ref[...], preferred_element_type=F32))


def _hyena_filter_spectrum(w1p, b1, w2, b2, w3, b3, wout, freq, cm, sm):
    def whole(a):
        return pl.BlockSpec(a.shape, lambda i: (0,) * a.ndim)

    tap_spec = pl.BlockSpec((FILT_RB, HY_WIDTH), lambda i: (i, 0))
    tap_shape = jax.ShapeDtypeStruct((SEQ, HY_WIDTH), BF16)
    params = (w1p, b1, w2, b2, w3, b3, wout, freq)
    s_hi, s_lo, d_hi, d_lo, hn = pl.pallas_call(
        _filter_taps_kernel,
        grid=(SEQ // FILT_RB,),
        in_specs=[whole(a) for a in params],
        out_specs=[tap_spec] * 4 + [pl.BlockSpec((1, HY_WIDTH), lambda i: (0, 0))],
        out_shape=[tap_shape] * 4 + [jax.ShapeDtypeStruct((1, HY_WIDTH), F32)],
        compiler_params=_cparams(("arbitrary",)),
        name="hyena_filter_taps",
    )(*params)
    full = pl.BlockSpec((SEQ, HY_WIDTH), lambda i: (0, 0))
    tab = pl.BlockSpec((FILT_RB, SEQ), lambda i: (i, 0))
    spec_out = pl.BlockSpec((FILT_RB, HY_WIDTH), lambda i: (i, 0))
    hr, hi = pl.pallas_call(
        _filter_spectrum_kernel,
        grid=(SEQ // FILT_RB,),
        in_specs=[tab, tab, full, full, full, full],
        out_specs=[spec_out, spec_out],
        out_shape=[jax.ShapeDtypeStruct((SEQ, HY_WIDTH), F32)] * 2,
        compiler_params=_cparams(("arbitrary",), VMEM_LIMIT),
        name="hyena_filter_spectrum",
    )(cm, sm, s_hi, s_lo, d_hi, d_lo)
    return hr, hi, hn


HY_CB = 256


def _conv3(p, w_ref, b_ref):
    n = p.shape[0]
    row = lax.broadcasted_iota(jnp.int32, p.shape, 0)
    prev = jnp.where(row == 0, 0.0, pltpu.roll(p, 1, 0))
    nxt = jnp.where(row == n - 1, 0.0, pltpu.roll(p, n - 1, 0))
    return prev * w_ref[0:1, :] + p * w_ref[1:2, :] + nxt * w_ref[2:3, :] + b_ref[...]


HY_RB = 256


def _hyena_kernel(x0_ref, x1_ref, v_ref, w0_ref, w1_ref, w2_ref, b0_ref, b1_ref, b2_ref,
                  cm_ref, sm_ref, hr_ref, hi_ref, hn_ref, skip_ref, o_ref,
                  u_scr, ub_scr, x0_scr, za_scr, zb_scr):
    x1c = _conv3(x1_ref[0].astype(F32), w1_ref, b1_ref)
    vc = _conv3(v_ref[0].astype(F32), w2_ref, b2_ref)
    u = vc * x1c
    u_scr[...] = u
    ub_scr[...] = u.astype(BF16)
    x0_scr[...] = _conv3(x0_ref[0].astype(F32), w0_ref, b0_ref)
    row = lax.broadcasted_iota(jnp.int32, u.shape, 0)
    xn = jnp.sum(u * (1 - 2 * (row & 1)).astype(F32), axis=0, keepdims=True)
    zn = xn * hn_ref[...] * (1.0 / N_FFT)

    def rows(r):
        return pl.ds(pl.multiple_of(r * HY_RB, HY_RB), HY_RB)

    def fwd(r, carry):
        rs = rows(r)
        ub = ub_scr[...]
        xr = jnp.dot(cm_ref[rs, :], ub, preferred_element_type=F32)
        xs = jnp.dot(sm_ref[rs, :], ub, preferred_element_type=F32)
        hr = hr_ref[rs, :]
        hi = hi_ref[rs, :]
        k = lax.broadcasted_iota(jnp.int32, xr.shape, 0) + r * HY_RB
        wk = jnp.where(k == 0, 1.0 / N_FFT, 2.0 / N_FFT)
        za_scr[rs, :] = ((xr * hr + xs * hi) * wk).astype(BF16)
        zb_scr[rs, :] = ((xs * hr - xr * hi) * wk).astype(BF16)
        return carry

    lax.fori_loop(0, SEQ // HY_RB, fwd, 0)

    def inv(r, carry):
        rs = rows(r)
        y = (jnp.dot(cm_ref[rs, :], za_scr[...], preferred_element_type=F32)
             + jnp.dot(sm_ref[rs, :], zb_scr[...], preferred_element_type=F32))
        tpos = lax.broadcasted_iota(jnp.int32, y.shape, 0)
        y = y + (1 - 2 * (tpos & 1)).astype(F32) * zn
        y = y + u_scr[rs, :] * skip_ref[...]
        o_ref[0, rs, :] = x0_scr[rs, :] * y
        return carry

    lax.fori_loop(0, SEQ // HY_RB, inv, 0)


def _hyena_mix(proj3, conv_w, conv_b, cm, sm, hr, hi, hn, skip):
    ncb = HY_WIDTH // HY_CB

    def part_spec(part):
        return pl.BlockSpec((1, SEQ, HY_CB), lambda b, c: (b, 0, part * ncb + c))

    def w_spec(part):
        return pl.BlockSpec((3, HY_CB), lambda b, c: (0, part * ncb + c))

    def b_spec(part):
        return pl.BlockSpec((1, HY_CB), lambda b, c: (0, part * ncb + c))

    tab_spec = pl.BlockSpec((SEQ, SEQ), lambda b, c: (0, 0), pipeline_mode=pl.Buffered(1))
    return pl.pallas_call(
        _hyena_kernel,
        grid=(BATCH, ncb),
        in_specs=[part_spec(0), part_spec(1), part_spec(2),
                  w_spec(0), w_spec(1), w_spec(2), b_spec(0), b_spec(1), b_spec(2),
                  tab_spec, tab_spec,
                  pl.BlockSpec((SEQ, HY_CB), lambda b, c: (0, c)),
                  pl.BlockSpec((SEQ, HY_CB), lambda b, c: (0, c)),
                  pl.BlockSpec((1, HY_CB), lambda b, c: (0, c)),
                  pl.BlockSpec((1, HY_CB), lambda b, c: (0, c))],
        out_specs=pl.BlockSpec((1, SEQ, HY_CB), lambda b, c: (b, 0, c)),
        out_shape=jax.ShapeDtypeStruct((BATCH, SEQ, HY_WIDTH), F32),
        scratch_shapes=[pltpu.VMEM((SEQ, HY_CB), F32), pltpu.VMEM((SEQ, HY_CB), BF16),
                        pltpu.VMEM((SEQ, HY_CB), F32), pltpu.VMEM((SEQ, HY_CB), BF16),
                        pltpu.VMEM((SEQ, HY_CB), BF16)],
        compiler_params=_cparams(("arbitrary", "arbitrary"), VMEM_LIMIT),
        name="hyena_mix",
    )(proj3, proj3, proj3, conv_w, conv_w, conv_w, conv_b, conv_b, conv_b,
      cm, sm, hr, hi, hn, skip)


def _rope(t, cos_ref, sina_ref, sinb_ref):
    half = ROPE_AXIS_DIM // 2
    return (t * cos_ref[...]
            + pltpu.roll(t, LANES - half, 1) * sina_ref[...]
            + pltpu.roll(t, half, 1) * sinb_ref[...])


def _nt_dot(a, b):
    return lax.dot_general(a, b, (((1,), (1,)), ((), ())), preferred_element_type=F32)


def _ret_kernel(q_ref, k_ref, v_ref, g_ref, cos_ref, sina_ref, sinb_ref, m_ref,
                xif_ref, xib_ref, zf_ref, zb_ref, gam_ref, s0f_ref, s0b_ref, gnw_ref, o_ref):
    nblk = SEQ // RET_BLOCK
    q = _rope(q_ref[0].astype(F32), cos_ref, sina_ref, sinb_ref)
    k = _rope(k_ref[0].astype(F32) * (RET_HEAD_DIM ** -0.5), cos_ref, sina_ref, sinb_ref)
    qb = q.astype(BF16)
    vb = v_ref[0]
    zf = zf_ref[0]
    zb = zb_ref[0]
    gf = gam_ref[0, 0:1, :]
    gb = gam_ref[0, 1:2, :]

    def blk(a, c):
        return a[c * RET_BLOCK:(c + 1) * RET_BLOCK]

    a_f, a_b = [], []
    for c in range(nblk):
        kc = blk(k, c)
        vc = blk(vb, c)
        a_f.append(jnp.dot((kc * zf).T.astype(BF16), vc, preferred_element_type=F32))
        a_b.append(jnp.dot((kc * zb).T.astype(BF16), vc, preferred_element_type=F32))
    s_f = [None] * nblk
    s_b = [None] * nblk
    s_f[0] = s0f_ref[0, 0]
    for c in range(1, nblk):
        s_f[c] = s_f[c - 1] * gf + a_f[c - 1]
    s_b[nblk - 1] = s0b_ref[0, 0]
    for c in range(nblk - 2, -1, -1):
        s_b[c] = s_b[c + 1] * gb + a_b[c + 1]

    m = m_ref[0]
    xif = xif_ref[0]
    xib = xib_ref[0]
    gnw = gnw_ref[...]
    for c in range(nblk):
        qc = blk(qb, c)
        kc = blk(k, c).astype(BF16)
        vc = blk(vb, c)
        p = (_nt_dot(qc, kc) * m).astype(BF16)
        o = jnp.dot(p, vc, preferred_element_type=F32)
        o = o + jnp.dot(qc, s_f[c].astype(BF16), preferred_element_type=F32) * xif
        o = o + jnp.dot(qc, s_b[c].astype(BF16), preferred_element_type=F32) * xib
        mu = jnp.mean(o, axis=-1, keepdims=True)
        oc = o - mu
        var = jnp.mean(oc * oc, axis=-1, keepdims=True)
        on = oc * lax.rsqrt(var + EPS) * gnw
        gc = g_ref[0, c * RET_BLOCK:(c + 1) * RET_BLOCK, :].astype(F32)
        o_ref[0, c * RET_BLOCK:(c + 1) * RET_BLOCK, :] = (_silu(gc) * on).astype(BF16)


def _retention(proj3, tabs, s0f, s0b, gn_w):
    cos_t, sina_t, sinb_t, m_t, xif_t, xib_t, zf_t, zb_t, gam_t = tabs
    c0 = RET_COL0 // RET_HEAD_DIM

    def qkvg(part):
        off = c0 + part * RET_HEADS
        return pl.BlockSpec((1, SEQ, RET_HEAD_DIM), lambda b, h: (b, 0, off + h))

    rope_spec = pl.BlockSpec((SEQ, RET_HEAD_DIM), lambda b, h: (0, 0))
    vec_spec = pl.BlockSpec((1, RET_BLOCK, LANES), lambda b, h: (h, 0, 0))
    st_spec = pl.BlockSpec((1, 1, RET_HEAD_DIM, RET_HEAD_DIM), lambda b, h: (b, h, 0, 0))
    return pl.pallas_call(
        _ret_kernel,
        grid=(BATCH, RET_HEADS),
        in_specs=[qkvg(0), qkvg(1), qkvg(2), qkvg(3),
                  rope_spec, rope_spec, rope_spec,
                  pl.BlockSpec((1, RET_BLOCK, RET_BLOCK), lambda b, h: (h, 0, 0)),
                  vec_spec, vec_spec, vec_spec, vec_spec,
                  pl.BlockSpec((1, 8, LANES), lambda b, h: (h, 0, 0)),
                  st_spec, st_spec,
                  pl.BlockSpec((1, RET_HEAD_DIM), lambda b, h: (0, h))],
        out_specs=pl.BlockSpec((1, SEQ, RET_HEAD_DIM), lambda b, h: (b, 0, h)),
        out_shape=jax.ShapeDtypeStruct((BATCH, SEQ, RET_WIDTH), BF16),
        compiler_params=_cparams(("arbitrary", "arbitrary"), VMEM_LIMIT),
        name="retention",
    )(proj3, proj3, proj3, proj3, cos_t, sina_t, sinb_t, m_t, xif_t, xib_t, zf_t, zb_t, gam_t,
      s0f, s0b, gn_w)


MIX_TM = 512


PAIRS_PER_GROUP = EXPERTS_PER_GROUP * (EXPERTS_PER_GROUP - 1) // 2
N_BUCKETS = N_GROUPS * PAIRS_PER_GROUP
ROW_W = D_MODEL + LANES
ROUTE_BUCKET, ROUTE_W_LO, ROUTE_W_HI = 0, 1, 2


def _mix_out_kernel(x_ref, yh_ref, yr_ref, hnw_ref, wo_ref, g1_ref, n2w_ref, sh2_ref, sc2_ref,
                    rw_ref, rb_ref, x1_ref, tp_ref):
    yh = _rms(yh_ref[...], hnw_ref[...]).astype(BF16)
    y = (jnp.dot(yh, wo_ref[0:HY_WIDTH, :], preferred_element_type=F32)
         + jnp.dot(yr_ref[...], wo_ref[HY_WIDTH:, :], preferred_element_type=F32))
    x1 = x_ref[...] + g1_ref[0] * y
    x1_ref[...] = x1
    t = _rms(x1, n2w_ref[...]) * (1.0 + sc2_ref[0]) + sh2_ref[0]
    tp_ref[:, 0:D_MODEL] = t

    logits = jnp.dot(t, rw_ref[...], precision=HIGHEST, preferred_element_type=F32) + rb_ref[...]
    lane = lax.broadcasted_iota(jnp.int32, logits.shape, 1).astype(F32)
    neg = -jnp.inf
    gmask = (lane >= N_EXPERTS) & (lane < N_EXPERTS + N_GROUPS)
    lg = jnp.where(gmask, logits, neg)
    mg = jnp.max(lg, axis=-1, keepdims=True)
    p_sel = 1.0 / jnp.sum(jnp.exp(lg - mg), axis=-1, keepdims=True)
    gi = jnp.min(jnp.where(lg == mg, lane, float(LANES)), axis=-1, keepdims=True) - N_EXPERTS
    emask = (lane >= gi * EXPERTS_PER_GROUP) & (lane < (gi + 1) * EXPERTS_PER_GROUP)
    le = jnp.where(emask, logits, neg)
    me = jnp.max(le, axis=-1, keepdims=True)
    ee = jnp.exp(le - me)
    pe = ee / jnp.sum(ee, axis=-1, keepdims=True)
    pe = jnp.where(emask, pe, -1.0)
    v1 = jnp.max(pe, axis=-1, keepdims=True)
    i1 = jnp.min(jnp.where(pe == v1, lane, float(LANES)), axis=-1, keepdims=True)
    pe2 = jnp.where(lane == i1, -1.0, pe)
    v2 = jnp.max(pe2, axis=-1, keepdims=True)
    i2 = jnp.min(jnp.where(pe2 == v2, lane, float(LANES)), axis=-1, keepdims=True)
    scale = p_sel / (v1 + v2)
    first_lo = i1 < i2
    e_lo = jnp.where(first_lo, i1, i2) - gi * EXPERTS_PER_GROUP
    e_hi = jnp.where(first_lo, i2, i1) - gi * EXPERTS_PER_GROUP
    w_lo = jnp.where(first_lo, v1, v2) * scale
    w_hi = jnp.where(first_lo, v2, v1) * scale
    pair = e_lo * (2 * EXPERTS_PER_GROUP - 1 - e_lo) * 0.5 + (e_hi - e_lo - 1.0)
    bucket = gi * PAIRS_PER_GROUP + pair
    tp_ref[:, D_MODEL:] = jnp.where(lane == ROUTE_BUCKET, bucket,
                                    jnp.where(lane == ROUTE_W_LO, w_lo,
                                              jnp.where(lane == ROUTE_W_HI, w_hi, 0.0)))


def _mix_out(x2, y_hy, y_ret, hy_norm_w, w_out_bf, mod3, norm2_w, rw, rb):
    tiles_per_seq = SEQ // MIX_TM

    def mod_spec(j):
        return pl.BlockSpec((1, 1, D_MODEL), lambda i: ((i // tiles_per_seq) * N_MOD + j, 0, 0))

    row_spec = pl.BlockSpec((MIX_TM, D_MODEL), lambda i: (i, 0))
    return pl.pallas_call(
        _mix_out_kernel,
        grid=(TOKENS // MIX_TM,),
        in_specs=[row_spec,
                  pl.BlockSpec((MIX_TM, HY_WIDTH), lambda i: (i, 0)),
                  pl.BlockSpec((MIX_TM, RET_WIDTH), lambda i: (i, 0)),
                  pl.BlockSpec((1, HY_WIDTH), lambda i: (0, 0)),
                  pl.BlockSpec((D_MODEL, D_MODEL), lambda i: (0, 0)),
                  mod_spec(2),
                  pl.BlockSpec((1, D_MODEL), lambda i: (0, 0)),
                  mod_spec(3), mod_spec(4),
                  pl.BlockSpec((D_MODEL, LANES), lambda i: (0, 0)),
                  pl.BlockSpec((1, LANES), lambda i: (0, 0))],
        out_specs=[row_spec, pl.BlockSpec((MIX_TM, ROW_W), lambda i: (i, 0))],
        out_shape=[jax.ShapeDtypeStruct((TOKENS, D_MODEL), F32),
                   jax.ShapeDtypeStruct((TOKENS, ROW_W), F32)],
        compiler_params=_cparams(("arbitrary",), VMEM_LIMIT),
        name="mix_out",
    )(x2, y_hy, y_ret, hy_norm_w, w_out_bf, mod3, norm2_w, mod3, mod3, rw, rb)


MOE_TM = 256
MOE_TB = 512
MOE_NB = TOKENS // MOE_TB
MOE_TILES = TOKENS // MOE_TM + N_BUCKETS
SUBLANES = 8
MOE_FILL = MOE_TM + SUBLANES
MOE_ROWS = (MOE_TILES + 2) * MOE_TM
META_TILE, META_NUSED, META_FILL = 0, 1, 2


def _positions_kernel(tp_ref, pos_ref, meta_ref, cnt_scr, off_scr, run_scr):
    phase = pl.program_id(0)
    blk = pl.program_id(1)
    bucket = tp_ref[:, ROUTE_BUCKET:ROUTE_BUCKET + 1]
    lane = lax.broadcasted_iota(jnp.int32, (MOE_TB, LANES), 1).astype(F32)
    onehot_t = (lane == bucket).astype(F32).T
    blk_cnt = jnp.sum(onehot_t, axis=1, keepdims=True)

    @pl.when((phase == 0) & (blk == 0))
    def _():
        cnt_scr[...] = jnp.zeros_like(cnt_scr)

    @pl.when(phase == 0)
    def _():
        cnt_scr[...] += blk_cnt

    @pl.when((phase == 1) & (blk == 0))
    def _():
        cnt = cnt_scr[...]
        padded = jnp.floor((cnt + (MOE_TM - 1)) * (1.0 / MOE_TM)) * MOE_TM
        r = lax.broadcasted_iota(jnp.int32, (LANES, LANES), 0)
        c = lax.broadcasted_iota(jnp.int32, (LANES, LANES), 1)
        strict_lower = (c < r).astype(F32)
        off = jnp.dot(strict_lower, padded, precision=HIGHEST, preferred_element_type=F32)
        off_scr[...] = off
        run_scr[...] = jnp.zeros_like(run_scr)
        end = off + padded
        tile_start = c.astype(F32) * MOE_TM
        is_bucket = r < N_BUCKETS
        tile_bucket = jnp.sum(jnp.where(is_bucket & (end <= tile_start), 1.0, 0.0), axis=0, keepdims=True)
        tile_bucket = jnp.minimum(tile_bucket, N_BUCKETS - 1.0)
        n_used = jnp.max(end, axis=0, keepdims=True) * (1.0 / MOE_TM)
        fill = (jnp.floor((off + cnt) * (1.0 / SUBLANES)) * SUBLANES).T[0:1, :]
        row = lax.broadcasted_iota(jnp.int32, (8, LANES), 0)
        meta = jnp.where(row == META_TILE, tile_bucket,
                         jnp.where(row == META_NUSED, n_used, jnp.where(row == META_FILL, fill, 0.0)))
        meta_ref[...] = meta.astype(jnp.int32)

    @pl.when(phase == 1)
    def _():
        ti = lax.broadcasted_iota(jnp.int32, (MOE_TB, MOE_TB), 0)
        tj = lax.broadcasted_iota(jnp.int32, (MOE_TB, MOE_TB), 1)
        before = (ti < tj).astype(BF16)
        seen = jnp.dot(onehot_t.astype(BF16), before, preferred_element_type=F32)
        base = off_scr[:, 0:1] + run_scr[:, 0:1]
        pos = jnp.sum(onehot_t * (seen + base), axis=0, keepdims=True)
        pos_ref[0] = pos.astype(jnp.int32)
        run_scr[...] += blk_cnt


def _moe_positions(tp):
    return pl.pallas_call(
        _positions_kernel,
        grid=(2, MOE_NB),
        in_specs=[pl.BlockSpec((MOE_TB, LANES), lambda p, i: (i, D_MODEL // LANES))],
        out_specs=[pl.BlockSpec((1, 1, MOE_TB), lambda p, i: (i * p, 0, 0)),
                   pl.BlockSpec((8, LANES), lambda p, i: (0, 0))],
        out_shape=[jax.ShapeDtypeStruct((MOE_NB, 1, MOE_TB), jnp.int32),
                   jax.ShapeDtypeStruct((8, LANES), jnp.int32)],
        scratch_shapes=[pltpu.VMEM((LANES, LANES), F32)] * 3,
        compiler_params=_cparams(("arbitrary", "arbitrary")),
        name="moe_positions",
    )(tp)


def _scatter_kernel(meta_ref, pos_ref, tp_ref, xs_ref, zero_scr, sem):
    @pl.when(pl.program_id(0) == 0)
    def _():
        zero_scr[...] = jnp.zeros_like(zero_scr)

        def tail_copy(tile):
            return pltpu.make_async_copy(
                zero_scr.at[pl.ds(0, MOE_TM), :],
                xs_ref.at[pl.ds(pl.multiple_of(tile * MOE_TM, MOE_TM), MOE_TM), :], sem)

        n_used = meta_ref[META_NUSED * LANES]
        lax.fori_loop(n_used, MOE_ROWS // MOE_TM, lambda t, c: (tail_copy(t).start(), c)[1], 0)
        lax.fori_loop(n_used, MOE_ROWS // MOE_TM, lambda t, c: (tail_copy(t).wait(), c)[1], 0)

        fills = [pltpu.make_async_copy(
            zero_scr,
            xs_ref.at[pl.ds(pl.multiple_of(meta_ref[META_FILL * LANES + b], SUBLANES), MOE_FILL), :], sem)
            for b in range(N_BUCKETS)]
        for f in fills:
            f.start()
        for f in fills:
            f.wait()

    def row_copy(j):
        return pltpu.make_async_copy(tp_ref.at[pl.ds(j, 1), :],
                                     xs_ref.at[pl.ds(pos_ref[0, 0, j], 1), :], sem)

    def issue(j, carry):
        row_copy(j).start()
        return carry

    lax.fori_loop(0, MOE_TB, issue, 0)
    pltpu.make_async_copy(tp_ref, xs_ref.at[pl.ds(0, MOE_TB), :], sem).wait()


def _moe_scatter(meta_flat, pos, tp):
    grid_spec = pltpu.PrefetchScalarGridSpec(
        num_scalar_prefetch=1,
        grid=(MOE_NB,),
        in_specs=[pl.BlockSpec((1, 1, MOE_TB), lambda i, m: (i, 0, 0), memory_space=pltpu.SMEM),
                  pl.BlockSpec((MOE_TB, ROW_W), lambda i, m: (i, 0))],
        out_specs=pl.BlockSpec(memory_space=pl.ANY),
        scratch_shapes=[pltpu.VMEM((MOE_FILL, ROW_W), F32), pltpu.SemaphoreType.DMA(())],
    )
    return pl.pallas_call(
        _scatter_kernel,
        grid_spec=grid_spec,
        out_shape=jax.ShapeDtypeStruct((MOE_ROWS, ROW_W), F32),
        compiler_params=_cparams(("arbitrary",)),
        name="moe_scatter",
    )(meta_flat, pos, tp)


def _expert_kernel(meta_ref, lo_ref, hi_ref, xs_ref, w1l_ref, w3l_ref, w2l_ref, w1h_ref, w3h_ref, w2h_ref,
                   ys_ref):
    used = pl.program_id(0) < meta_ref[META_NUSED * LANES]

    @pl.when(jnp.logical_not(used))
    def _():
        ys_ref[...] = jnp.zeros_like(ys_ref)

    @pl.when(used)
    def _():
        x = xs_ref[:, 0:D_MODEL].astype(BF16)
        w_lo = xs_ref[:, D_MODEL + ROUTE_W_LO:D_MODEL + ROUTE_W_LO + 1]
        w_hi = xs_ref[:, D_MODEL + ROUTE_W_HI:D_MODEL + ROUTE_W_HI + 1]

        def hidden(w1_ref, w3_ref, w):
            a = jnp.dot(x, w1_ref[0], preferred_element_type=F32)
            u = jnp.dot(x, w3_ref[0], preferred_element_type=F32)
            return (_silu(a) * u * w).astype(BF16)

        ys_ref[...] = (jnp.dot(hidden(w1l_ref, w3l_ref, w_lo), w2l_ref[0], preferred_element_type=F32)
                       + jnp.dot(hidden(w1h_ref, w3h_ref, w_hi), w2h_ref[0], preferred_element_type=F32))


def _moe_experts(meta_flat, lo_tab, hi_tab, xs, w1, w3, w2):
    def tile(i, m):
        return jnp.minimum(i, m[META_NUSED * LANES] - 1)

    def up_spec(tab_idx):
        return pl.BlockSpec((1, D_MODEL, EXPERT_HIDDEN),
                            lambda i, m, lo, hi: ((lo, hi)[tab_idx][m[tile(i, m)]], 0, 0))

    def down_spec(tab_idx):
        return pl.BlockSpec((1, EXPERT_HIDDEN, D_MODEL),
                            lambda i, m, lo, hi: ((lo, hi)[tab_idx][m[tile(i, m)]], 0, 0))

    grid_spec = pltpu.PrefetchScalarGridSpec(
        num_scalar_prefetch=3,
        grid=(MOE_TILES,),
        in_specs=[pl.BlockSpec((MOE_TM, ROW_W), lambda i, m, lo, hi: (tile(i, m), 0)),
                  up_spec(0), up_spec(0), down_spec(0), up_spec(1), up_spec(1), down_spec(1)],
        out_specs=pl.BlockSpec((MOE_TM, D_MODEL), lambda i, m, lo, hi: (i, 0)),
    )
    return pl.pallas_call(
        _expert_kernel,
        grid_spec=grid_spec,
        out_shape=jax.ShapeDtypeStruct((MOE_TILES * MOE_TM, D_MODEL), F32),
        compiler_params=_cparams(("arbitrary",), VMEM_LIMIT),
        name="moe_experts",
    )(meta_flat, lo_tab, hi_tab, xs, w1, w3, w2, w1, w3, w2)


def _combine_kernel(pos_ref, nxt_ref, ys_ref, x1_ref, g2_ref, fnw_ref, o_ref, buf, sem):
    i = pl.program_id(0)
    slot = i % 2

    def issue(p_ref, s):
        def body(j, carry):
            pltpu.make_async_copy(ys_ref.at[pl.ds(p_ref[0, 0, j], 1), :],
                                  buf.at[s, pl.ds(j, 1), :], sem.at[s]).start()
            return carry
        lax.fori_loop(0, MOE_TB, body, 0)

    @pl.when(i == 0)
    def _():
        issue(pos_ref, 0)

    @pl.when(i + 1 < pl.num_programs(0))
    def _():
        issue(nxt_ref, 1 - slot)

    pltpu.make_async_copy(ys_ref.at[pl.ds(0, MOE_TB), :], buf.at[slot], sem.at[slot]).wait()
    o_ref[...] = _rms(x1_ref[...] + g2_ref[0] * buf[slot], fnw_ref[...])


def _moe_combine(pos, ys, x1, mod3, final_w):
    blocks_per_seq = SEQ // MOE_TB
    row_spec = pl.BlockSpec((MOE_TB, D_MODEL), lambda i: (i, 0))
    return pl.pallas_call(
        _combine_kernel,
        grid=(MOE_NB,),
        in_specs=[pl.BlockSpec((1, 1, MOE_TB), lambda i: (i, 0, 0), memory_space=pltpu.SMEM),
                  pl.BlockSpec((1, 1, MOE_TB), lambda i: (jnp.minimum(i + 1, MOE_NB - 1), 0, 0),
                               memory_space=pltpu.SMEM),
                  pl.BlockSpec(memory_space=pl.ANY),
                  row_spec,
                  pl.BlockSpec((1, 1, D_MODEL), lambda i: ((i // blocks_per_seq) * N_MOD + 5, 0, 0)),
                  pl.BlockSpec((1, D_MODEL), lambda i: (0, 0))],
        out_specs=row_spec,
        out_shape=jax.ShapeDtypeStruct((TOKENS, D_MODEL), F32),
        scratch_shapes=[pltpu.VMEM((2, MOE_TB, D_MODEL), F32), pltpu.SemaphoreType.DMA((2,))],
        compiler_params=_cparams(("arbitrary",), VMEM_LIMIT),
        name="moe_combine",
    )(pos, pos, ys, x1, mod3, final_w)


def _bucket_expert_tables():
    lo, hi = [], []
    for g in range(N_GROUPS):
        for a in range(EXPERTS_PER_GROUP):
            for b in range(a + 1, EXPERTS_PER_GROUP):
                lo.append(g * EXPERTS_PER_GROUP + a)
                hi.append(g * EXPERTS_PER_GROUP + b)
    return jnp.asarray(lo, jnp.int32), jnp.asarray(hi, jnp.int32)


def _dft_tables():
    k = lax.broadcasted_iota(jnp.int32, (SEQ, SEQ), 0)
    t = lax.broadcasted_iota(jnp.int32, (SEQ, SEQ), 1)
    ang = ((k * t) & (N_FFT - 1)).astype(F32) * (2.0 * math.pi / N_FFT)
    return jnp.cos(ang).astype(BF16), jnp.sin(ang).astype(BF16)


def _log_gammas(offset):
    return jnp.log1p(-jnp.exp2(-(offset + jnp.arange(RET_HEADS, dtype=F32))))


def _lane_bcast(a):
    return jnp.broadcast_to(a[..., None], a.shape + (LANES,))


def _retention_tables():
    n = jnp.arange(SEQ)
    rows = (n // GRID_W).astype(F32)
    cols = (n % GRID_W).astype(F32)
    n_freq = ROPE_AXIS_DIM // 2
    freqs = ROPE_BASE ** (-jnp.arange(n_freq, dtype=F32) / n_freq)
    ar = rows[:, None] * freqs[None, :]
    ac = cols[:, None] * freqs[None, :]
    zeros = jnp.zeros((SEQ, n_freq), F32)
    rest = RET_HEAD_DIM - 2 * ROPE_AXIS_DIM
    cos_t = jnp.concatenate([jnp.cos(ar), jnp.cos(ar), jnp.cos(ac), jnp.cos(ac),
                             jnp.ones((SEQ, rest), F32)], axis=-1)
    sina_t = jnp.concatenate([-jnp.sin(ar), zeros, -jnp.sin(ac), zeros,
                              jnp.zeros((SEQ, rest), F32)], axis=-1)
    sinb_t = jnp.concatenate([zeros, jnp.sin(ar), zeros, jnp.sin(ac),
                              jnp.zeros((SEQ, rest), F32)], axis=-1)
    lgf = _log_gammas(RET_DECAY_OFFSET_F)
    lgb = _log_gammas(RET_DECAY_OFFSET_B)
    i = jnp.arange(RET_BLOCK, dtype=F32)
    diff = i[:, None] - i[None, :]
    m_t = jnp.where(diff > 0, jnp.exp(jnp.maximum(diff, 0.0)[None] * lgf[:, None, None]),
                    jnp.where(diff < 0, jnp.exp(jnp.maximum(-diff, 0.0)[None] * lgb[:, None, None]), 2.0))
    xif = _lane_bcast(jnp.exp((i + 1.0)[None, :] * lgf[:, None]))
    xib = _lane_bcast(jnp.exp((RET_BLOCK - i)[None, :] * lgb[:, None]))
    zf = _lane_bcast(jnp.exp((RET_BLOCK - 1.0 - i)[None, :] * lgf[:, None]))
    zb = _lane_bcast(jnp.exp(i[None, :] * lgb[:, None]))
    gam = jnp.zeros((RET_HEADS, 8, LANES), F32)
    gam = gam.at[:, 0, :].set(jnp.exp(RET_BLOCK * lgf)[:, None])
    gam = gam.at[:, 1, :].set(jnp.exp(RET_BLOCK * lgb)[:, None])
    j = jnp.arange(CTX_LEN, dtype=F32)
    wf = _lane_bcast(jnp.exp((CTX_LEN - 1.0 - j)[None, :] * lgf[:, None]))
    wb = _lane_bcast(jnp.exp(j[None, :] * lgb[:, None]))
    return (cos_t, sina_t, sinb_t, m_t, xif, xib, zf, zb, gam), wf, wb


def kernel(x, c, ctx, c_ctx, ada_w, ada_b, norm1_w, w_in, hy_conv_w, hy_conv_b, hy_f_w1, hy_f_b1,
           hy_f_w2, hy_f_b2, hy_f_w3, hy_f_b3, hy_f_wout, hy_f_freq, hy_skip, hy_out_norm, ret_gn_w,
           w_out, norm2_w, router_g_w, router_g_b, router_e_w, router_e_b, exp_w1, exp_w3, exp_w2,
           final_norm_w):
    x2 = x.reshape(TOKENS, D_MODEL)

    c_rows = jnp.concatenate([c, c_ctx[None, :], jnp.zeros((MOD_ROWS - BATCH - 1, D_MODEL), F32)], axis=0)
    mod = _ada_mod(c_rows, ada_w[0], ada_b[0])
    mod3 = mod.reshape(MOD_ROWS * N_MOD, 1, D_MODEL)

    ret_tabs, wf_tab, wb_tab = _retention_tables()
    cm, sm = _dft_tables()

    w_in_bf = w_in[0].astype(BF16)
    k0 = RET_COL0 + RET_WIDTH
    s0f, s0b = _context_states(ctx, norm1_w, mod3, w_in_bf[:, k0:k0 + 2 * RET_WIDTH], wf_tab, wb_tab)

    proj = _in_proj(x2, norm1_w, mod3, w_in_bf)
    proj3 = proj.reshape(BATCH, SEQ, PROJ_COLS)

    w1p = jnp.zeros((LANES, HY_FILTER_HIDDEN), F32).at[:HY_EMB_DIM].set(hy_f_w1[0])
    hr, hi, hn = _hyena_filter_spectrum(w1p, hy_f_b1, hy_f_w2[0], hy_f_b2, hy_f_w3[0], hy_f_b3,
                                        hy_f_wout[0], hy_f_freq, cm, sm)
    y_hy = _hyena_mix(proj3, hy_conv_w[0], hy_conv_b, cm, sm, hr, hi, hn, hy_skip)
    y_ret = _retention(proj3, ret_tabs, s0f, s0b, ret_gn_w)

    rw = jnp.zeros((D_MODEL, LANES), F32)
    rw = rw.at[:, :N_EXPERTS].set(router_e_w[0]).at[:, N_EXPERTS:N_EXPERTS + N_GROUPS].set(router_g_w[0])
    rb = jnp.zeros((1, LANES), F32)
    rb = rb.at[0, :N_EXPERTS].set(router_e_b[0]).at[0, N_EXPERTS:N_EXPERTS + N_GROUPS].set(router_g_b[0])
    x1, tp = _mix_out(x2, y_hy.reshape(TOKENS, HY_WIDTH), y_ret.reshape(TOKENS, RET_WIDTH),
                      hy_out_norm, w_out[0].astype(BF16), mod3, norm2_w, rw, rb)

    pos, meta = _moe_positions(tp)
    meta_flat = meta.reshape(-1)
    xs = _moe_scatter(meta_flat, pos, tp)
    w1 = exp_w1[0].reshape(N_EXPERTS, D_MODEL, EXPERT_HIDDEN).astype(BF16)
    w3 = exp_w3[0].reshape(N_EXPERTS, D_MODEL, EXPERT_HIDDEN).astype(BF16)
    w2 = exp_w2[0].reshape(N_EXPERTS, EXPERT_HIDDEN, D_MODEL).astype(BF16)
    lo_tab, hi_tab = _bucket_expert_tables()
    ys = _moe_experts(meta_flat, lo_tab, hi_tab, xs, w1, w3, w2)
    out = _moe_combine(pos, ys, x1, mod3, final_norm_w.reshape(1, D_MODEL))
    return out.reshape(BATCH, SEQ, D_MODEL)
```
